```python
import jax, jax.numpy as jnp
from jax import lax
import numpy as np

D_MODEL = 1024
BATCH = 8
SEQ = 2048
DEPTH = 2

CHUNK = 64
PLE_DIM = 256
N_A = DEPTH // 2
N_B = DEPTH - N_A
POOL_WINDOWS = (2, 4, 8, 16)
N_POOL_GROUPS = len(POOL_WINDOWS)
POOL_GROUP_DIM = D_MODEL // N_POOL_GROUPS
POOL_WMAX = max(POOL_WINDOWS)
SB_HEADS = 16
SB_HEAD_DIM = D_MODEL // SB_HEADS
SB_SCALE = SB_HEAD_DIM ** -0.5
Q_BLOCK = 128
EPS = 1e-6

kernel_name = "yoco_pool_stickbreaking_hybrid"


def rms_norm(x, g):
    xf = x.astype(jnp.float32)
    y = xf * lax.rsqrt(jnp.mean(xf * xf, axis=-1, keepdims=True) + EPS)
    return (y * g.astype(jnp.float32)).astype(x.dtype)


def pool_mixer(h, w_in, w_group, scale, w_out):
    B, S, _ = h.shape
    u, z = jnp.split(h @ w_in, 2, axis=-1)
    u = u.reshape(B, S, N_POOL_GROUPS, POOL_GROUP_DIM)
    uf = u.astype(jnp.float32)
    cs = jnp.cumsum(uf, axis=1)
    cs_ext = jnp.pad(cs, ((0, 0), (POOL_WMAX, 0), (0, 0), (0, 0)))
    pos1 = jnp.arange(1, S + 1)
    means = []
    for g, w in enumerate(POOL_WINDOWS):
        lo = cs_ext[:, POOL_WMAX - w: POOL_WMAX - w + S, g]
        cnt = jnp.minimum(pos1, w).astype(jnp.float32)[None, :, None]
        means.append((cs[:, :, g] - lo) / cnt)
    pooled = (jnp.stack(means, axis=2) - uf).astype(h.dtype)
    mixed = jnp.einsum('bsgc,gcd->bsgd', pooled, w_group).reshape(B, S, D_MODEL) * scale
    return (mixed * jax.nn.silu(z)) @ w_out


def split_heads(t):
    B, S, _ = t.shape
    return t.reshape(B, S, SB_HEADS, SB_HEAD_DIM).transpose(0, 2, 1, 3)


def stick_breaking_attention(q, k, v):
    S = q.shape[2]
    outs = []
    for blk in range(S // Q_BLOCK):
        t0 = blk * Q_BLOCK
        L = t0 + Q_BLOCK
        logits = jnp.einsum('bhtd,bhsd->bhts', q[:, :, t0:L], k[:, :, :L]).astype(jnp.float32) * SB_SCALE
        t_idx = t0 + jnp.arange(Q_BLOCK)[:, None]
        s_idx = jnp.arange(L)[None, :]
        mask = s_idx < t_idx
        log_keep = jnp.where(mask, jax.nn.log_sigmoid(-logits), 0.0)
        later = lax.cumsum(log_keep, axis=log_keep.ndim - 1, reverse=True) - log_keep
        log_a = jax.nn.log_sigmoid(logits) + later
        a = jnp.where(mask, jnp.exp(log_a), 0.0)
        outs.append(jnp.einsum('bhts,bhsd->bhtd', a.astype(v.dtype), v[:, :, :L]))
    return jnp.concatenate(outs, axis=2)


def setup_inputs(seed: int = 0) -> dict:
    key = jax.random.key(seed)
    ks = jax.random.split(key, 20)
    D, C = D_MODEL, POOL_GROUP_DIM
    f32 = jnp.float32

    def nrm(k, shape, fan_in, gain=1.0):
        return jax.random.normal(k, shape, f32) * (gain * fan_in ** -0.5)

    def gain(k, shape):
        return 1.0 + 0.05 * jax.random.normal(k, shape, f32)

    return {
        "x": jax.random.normal(ks[0], (BATCH, SEQ, D), f32),
        "p": jax.random.normal(ks[1], (DEPTH, BATCH, SEQ, PLE_DIM), f32),
        "a_norm": gain(ks[2], (N_A, D)),
        "a_w_in": nrm(ks[3], (N_A, D, 2 * D), D),
        "a_w_group": nrm(ks[4], (N_A, N_POOL_GROUPS, C, C), C),
        "a_scale": gain(ks[5], (N_A, D)),
        "a_w_out": nrm(ks[6], (N_A, D, D), D, 0.5),
        "kv_norm": gain(ks[7], (D,)),
        "w_kv": nrm(ks[8], (D, 2 * D), D),
        "k_norm": gain(ks[9], (SB_HEAD_DIM,)),
        "b_norm": gain(ks[10], (N_B, D)),
        "b_w_in": nrm(ks[11], (N_B, D, 2 * D), D),
        "b_q_norm": gain(ks[12], (N_B, SB_HEAD_DIM)),
        "b_w_out": nrm(ks[13], (N_B, D, D), D, 0.5),
        "ple_w": nrm(ks[14], (DEPTH, PLE_DIM, D), PLE_DIM, 0.5),
        "ple_gate_w": nrm(ks[15], (DEPTH, D, D), D),
    }


def reference(x, p, a_norm, a_w_in, a_w_group, a_scale, a_w_out, kv_norm, w_kv, k_norm,
              b_norm, b_w_in, b_q_norm, b_w_out, ple_w, ple_gate_w):
    B, S, _ = x.shape
    k_sh = v_sh = None
    for i in range(DEPTH):
        if i < N_A:
            h = rms_norm(x, a_norm[i])
            x = x + pool_mixer(h, a_w_in[i], a_w_group[i], a_scale[i], a_w_out[i])
        else:
            j = i - N_A
            if j == 0:
                kv_in = rms_norm(x, kv_norm)
                k_all, v_all = jnp.split(kv_in @ w_kv, 2, axis=-1)
                k_sh = rms_norm(split_heads(k_all), k_norm)
                v_sh = split_heads(v_all)
            h = rms_norm(x, b_norm[j])
            q, z = jnp.split(h @ b_w_in[j], 2, axis=-1)
            q = rms_norm(split_heads(q), b_q_norm[j])
            o = stick_breaking_attention(q, k_sh, v_sh)
            o = o.transpose(0, 2, 1, 3).reshape(B, S, D_MODEL)
            x = x + (o * jax.nn.silu(z)) @ b_w_out[j]
        x = x + (p[i] @ ple_w[i]) * jax.nn.sigmoid(x @ ple_gate_w[i])
    return x
```

```python
import functools

import jax
import jax.numpy as jnp
from jax import lax
from jax.experimental import pallas as pl
from jax.experimental.pallas import tpu as pltpu

D_MODEL = 1024
PLE_DIM = 256
POOL_WINDOWS = (2, 4, 8, 16)
POOL_GROUP_DIM = D_MODEL // len(POOL_WINDOWS)
POOL_HALO = 16
SB_HEADS = 16
SB_HEAD_DIM = D_MODEL // SB_HEADS
SB_SCALE = SB_HEAD_DIM ** -0.5
EPS = 1e-6

LANES = 128
HEADS_PER_SLAB = LANES // SB_HEAD_DIM
N_SLABS = D_MODEL // LANES

ROW_TILE = 512
Q_TILE = 256
K_TILE = 256
VMEM_LIMIT_BYTES = 56 * 1024 * 1024

F32 = jnp.float32
BF16 = jnp.bfloat16


def _sigmoid(t):
    return 1.0 / (1.0 + jnp.exp(-t))


def _rms_scale(t):
    return lax.rsqrt(jnp.mean(t * t, axis=-1, keepdims=True) + EPS)


def _head_rms(t):
    rows = t.shape[0]
    lo = lax.broadcasted_iota(jnp.int32, (rows, LANES), 1) < SB_HEAD_DIM
    outs = []
    for c in range(N_SLABS):
        blk = t[:, c * LANES:(c + 1) * LANES]
        sq = blk * blk
        s_lo = jnp.sum(jnp.where(lo, sq, 0.0), axis=-1, keepdims=True)
        s_hi = jnp.sum(jnp.where(lo, 0.0, sq), axis=-1, keepdims=True)
        ms = jnp.where(lo, s_lo, s_hi) * (1.0 / SB_HEAD_DIM)
        outs.append(blk * lax.rsqrt(ms + EPS))
    return jnp.concatenate(outs, axis=-1)


def _layer0_kernel(x_ref, p_ref, a_norm_ref, a_scale_ref, kv_norm_ref, b_norm_ref, k_gain_ref, q_gain_ref,
                   w_in_ref, w_group_ref, w_out_ref, ple_w_ref, gate_w_ref, w_kv_ref, bw_in_ref,
                   x1_ref, q_ref, k_ref, v_ref, z_ref, halo_ref):
    t = pl.program_id(1)
    rows = x_ref.shape[1]

    @pl.when(t == 0)
    def _():
        halo_ref[...] = jnp.zeros_like(halo_ref)

    x = x_ref[0]
    h = x * _rms_scale(x) * a_norm_ref[...]
    uz = jnp.dot(h.astype(BF16), w_in_ref[...], preferred_element_type=F32)
    u = uz[:, :D_MODEL]
    z = uz[:, D_MODEL:]

    ext = jnp.concatenate([halo_ref[...], u], axis=0)
    halo_ref[...] = u[rows - POOL_HALO:, :]
    pos1 = t * rows + lax.broadcasted_iota(jnp.int32, (rows, 1), 0) + 1
    mixed = []
    for g, w in enumerate(POOL_WINDOWS):
        e = ext[:, g * POOL_GROUP_DIM:(g + 1) * POOL_GROUP_DIM]
        s = e
        shift = 1
        while shift < w:
            s = s + pltpu.roll(s, shift, axis=0)
            shift *= 2
        inv_cnt = 1.0 / jnp.minimum(pos1, w).astype(F32)
        pooled = s[POOL_HALO:, :] * inv_cnt - e[POOL_HALO:, :]
        mixed.append(jnp.dot(pooled.astype(BF16), w_group_ref[g], preferred_element_type=F32))
    mixed = jnp.concatenate(mixed, axis=-1) * a_scale_ref[...]
    gated = mixed * (z * _sigmoid(z))
    x = x + jnp.dot(gated.astype(BF16), w_out_ref[...], preferred_element_type=F32)

    pe = jnp.dot(p_ref[0, 0].astype(BF16), ple_w_ref[...], preferred_element_type=F32)
    gate = _sigmoid(jnp.dot(x.astype(BF16), gate_w_ref[...], preferred_element_type=F32))
    x = x + pe * gate
    x1_ref[0] = x

    xn = x * _rms_scale(x)
    kv = jnp.dot((xn * kv_norm_ref[...]).astype(BF16), w_kv_ref[...], preferred_element_type=F32)
    qz = jnp.dot((xn * b_norm_ref[...]).astype(BF16), bw_in_ref[...], preferred_element_type=F32)
    k = (_head_rms(kv[:, :D_MODEL]) * k_gain_ref[...]).astype(BF16)
    v = kv[:, D_MODEL:].astype(BF16)
    q = (_head_rms(qz[:, :D_MODEL]) * q_gain_ref[...] * SB_SCALE).astype(BF16)
    z_ref[0] = qz[:, D_MODEL:]
    for c in range(N_SLABS):
        sl = slice(c * LANES, (c + 1) * LANES)
        q_ref[0, c] = q[:, sl]
        k_ref[0, c] = k[:, sl]
        v_ref[0, c] = v[:, sl]


def _attn_kernel(q_ref, k_ref, v_ref, z_ref, x1_ref, p_ref, tri_ref, w_out_ref, ple_w_ref, gate_w_ref,
                 out_ref, o_scr):
    i = pl.program_id(1)
    tq = q_ref.shape[2]
    lo = lax.broadcasted_iota(jnp.int32, (1, LANES), 1) < SB_HEAD_DIM
    row = lax.broadcasted_iota(jnp.int32, (tq, K_TILE), 0)
    col = lax.broadcasted_iota(jnp.int32, (tq, K_TILE), 1)
    causal = col < row
    tri = tri_ref[...]

    def key_block(slab, j, qs, carry, diagonal):
        c_sum, acc = carry
        start = pl.multiple_of(j * K_TILE, K_TILE)
        k2 = k_ref[0, slab, pl.ds(start, K_TILE), :]
        v2 = v_ref[0, slab, pl.ds(start, K_TILE), :]
        zero = jnp.zeros_like(v2)
        vs = (jnp.where(lo, v2, zero), jnp.where(lo, zero, v2))
        new_c = []
        for hh in range(HEADS_PER_SLAB):
            logits = lax.dot_general(qs[hh], k2, (((1,), (1,)), ((), ())), preferred_element_type=F32)
            sp = jnp.maximum(logits, 0.0) + jnp.log(1.0 + jnp.exp(-jnp.abs(logits)))
            if diagonal:
                sp_keep = jnp.where(causal, sp, 0.0)
            else:
                sp_keep = sp
            hi = sp_keep.astype(BF16)
            lo_part = (sp_keep - hi.astype(F32)).astype(BF16)
            later = (jnp.dot(hi, tri, preferred_element_type=F32)
                     + jnp.dot(lo_part, tri, preferred_element_type=F32))
            log_a = logits - sp - later - c_sum[hh]
            a = jnp.exp(log_a)
            if diagonal:
                a = jnp.where(causal, a, 0.0)
            acc = acc + jnp.dot(a.astype(BF16), vs[hh], preferred_element_type=F32)
            new_c.append(c_sum[hh] + jnp.sum(sp_keep, axis=-1, keepdims=True))
        return tuple(new_c), acc

    def slab_body(slab, _):
        q2 = q_ref[0, slab]
        zero = jnp.zeros_like(q2)
        qs = (jnp.where(lo, q2, zero), jnp.where(lo, zero, q2))
        carry = (tuple(jnp.zeros((tq, 1), F32) for _ in range(HEADS_PER_SLAB)), jnp.zeros((tq, LANES), F32))
        carry = key_block(slab, i, qs, carry, True)
        carry = lax.fori_loop(0, i, lambda jj, c: key_block(slab, i - 1 - jj, qs, c, False), carry)
        o_scr[slab] = carry[1]
        return 0

    lax.fori_loop(0, N_SLABS, slab_body, 0)

    o = jnp.concatenate([o_scr[c] for c in range(N_SLABS)], axis=-1)
    z = z_ref[0]
    gated = o * (z * _sigmoid(z))
    x = x1_ref[0] + jnp.dot(gated.astype(BF16), w_out_ref[...], preferred_element_type=F32)
    pe = jnp.dot(p_ref[0, 0].astype(BF16), ple_w_ref[...], preferred_element_type=F32)
    gate = _sigmoid(jnp.dot(x.astype(BF16), gate_w_ref[...], preferred_element_type=F32))
    out_ref[0] = x + pe * gate


def _resident(shape):
    return pl.BlockSpec(shape, lambda b, t: (0,) * len(shape), pipeline_mode=pl.Buffered(1))


def kernel(x, p, a_norm, a_w_in, a_w_group, a_scale, a_w_out, kv_norm, w_kv, k_norm, b_norm, b_w_in, b_q_norm,
           b_w_out, ple_w, ple_gate_w):
    B, S, D = x.shape
    assert D == D_MODEL and S % ROW_TILE == 0 and S % Q_TILE == 0 and Q_TILE == K_TILE
    assert a_norm.shape[0] == 1 and b_norm.shape[0] == 1 and p.shape[0] == 2

    row = lambda a: a.reshape(1, D_MODEL).astype(F32)
    per_head = lambda a: jnp.tile(a.astype(F32), SB_HEADS).reshape(1, D_MODEL)
    wb = lambda a: a.astype(BF16)

    slab_shape = jax.ShapeDtypeStruct((B, N_SLABS, S, LANES), BF16)
    tok_spec = pl.BlockSpec((1, ROW_TILE, D_MODEL), lambda b, t: (b, t, 0))
    slab_spec = pl.BlockSpec((1, N_SLABS, ROW_TILE, LANES), lambda b, t: (b, 0, t, 0))
    vec_spec = _resident((1, D_MODEL))

    x1, q, k, v, z = pl.pallas_call(
        _layer0_kernel,
        grid=(B, S // ROW_TILE),
        in_specs=[
            tok_spec,
            pl.BlockSpec((1, 1, ROW_TILE, PLE_DIM), lambda b, t: (0, b, t, 0)),
            vec_spec, vec_spec, vec_spec, vec_spec, vec_spec, vec_spec,
            _resident((D_MODEL, 2 * D_MODEL)),
            _resident((len(POOL_WINDOWS), POOL_GROUP_DIM, POOL_GROUP_DIM)),
            _resident((D_MODEL, D_MODEL)),
            _resident((PLE_DIM, D_MODEL)),
            _resident((D_MODEL, D_MODEL)),
            _resident((D_MODEL, 2 * D_MODEL)),
            _resident((D_MODEL, 2 * D_MODEL)),
        ],
        out_specs=[tok_spec, slab_spec, slab_spec, slab_spec, tok_spec],
        out_shape=[jax.ShapeDtypeStruct((B, S, D_MODEL), F32), slab_shape, slab_shape, slab_shape,
                   jax.ShapeDtypeStruct((B, S, D_MODEL), F32)],
        scratch_shapes=[pltpu.VMEM((POOL_HALO, D_MODEL), F32)],
        compiler_params=pltpu.CompilerParams(
            dimension_semantics=("arbitrary", "arbitrary"), vmem_limit_bytes=VMEM_LIMIT_BYTES),
        name="layer0_and_projections",
    )(x, p, row(a_norm[0]), row(a_scale[0]), row(kv_norm), row(b_norm[0]), per_head(k_norm), per_head(b_q_norm[0]),
      wb(a_w_in[0]), wb(a_w_group[0]), wb(a_w_out[0]), wb(ple_w[0]), wb(ple_gate_w[0]), wb(w_kv), wb(b_w_in[0]))

    tri = (lax.broadcasted_iota(jnp.int32, (K_TILE, K_TILE), 0)
           > lax.broadcasted_iota(jnp.int32, (K_TILE, K_TILE), 1)).astype(BF16)
    qtok_spec = pl.BlockSpec((1, Q_TILE, D_MODEL), lambda b, i: (b, i, 0))
    seq_spec = pl.BlockSpec((1, N_SLABS, S, LANES), lambda b, i: (b, 0, 0, 0))

    return pl.pallas_call(
        _attn_kernel,
        grid=(B, S // Q_TILE),
        in_specs=[
            pl.BlockSpec((1, N_SLABS, Q_TILE, LANES), lambda b, i: (b, 0, i, 0)),
            seq_spec, seq_spec,
            qtok_spec, qtok_spec,
            pl.BlockSpec((1, 1, Q_TILE, PLE_DIM), lambda b, i: (1, b, i, 0)),
            _resident((K_TILE, K_TILE)),
            _resident((D_MODEL, D_MODEL)),
            _resident((PLE_DIM, D_MODEL)),
            _resident((D_MODEL, D_MODEL)),
        ],
        out_specs=qtok_spec,
        out_shape=jax.ShapeDtypeStruct((B, S, D_MODEL), F32),
        scratch_shapes=[pltpu.VMEM((N_SLABS, Q_TILE, LANES), F32)],
        compiler_params=pltpu.CompilerParams(
            dimension_semantics=("arbitrary", "arbitrary"), vmem_limit_bytes=VMEM_LIMIT_BYTES),
        name="stick_breaking_layer",
    )(q, k, v, z, x1, p, tri, wb(b_w_out[0]), wb(ple_w[1]), wb(ple_gate_w[1]))
```

```python
import jax
import jax.numpy as jnp
from jax import lax
from jax.experimental import pallas as pl
from jax.experimental.pallas import tpu as pltpu

D_MODEL = 1024
PLE_DIM = 256
POOL_WINDOWS = (2, 4, 8, 16)
POOL_GROUP_DIM = D_MODEL // len(POOL_WINDOWS)
POOL_HALO = 16
SB_HEADS = 16
SB_HEAD_DIM = D_MODEL // SB_HEADS
SB_SCALE = SB_HEAD_DIM ** -0.5
EPS = 1e-6
LOG2_E = 1.4426950408889634

LANES = 128
HEADS_PER_SLAB = LANES // SB_HEAD_DIM
N_SLABS = D_MODEL // LANES

ROW_TILE = 512
Q_TILE = 256
K_TILE = 256
SLABS_PER_ITEM = 2
HEADS_PER_ITEM = SLABS_PER_ITEM * HEADS_PER_SLAB
GROUPS = N_SLABS // SLABS_PER_ITEM
MASKED_LOG2 = -1e30
VMEM_LIMIT_BYTES = 56 * 1024 * 1024

F32 = jnp.float32
BF16 = jnp.bfloat16


def _sigmoid(t):
    return 1.0 / (1.0 + jnp.exp(-t))


def _rms_scale(t):
    return lax.rsqrt(jnp.mean(t * t, axis=-1, keepdims=True) + EPS)


def _head_rms(t):
    rows = t.shape[0]
    lo = lax.broadcasted_iota(jnp.int32, (rows, LANES), 1) < SB_HEAD_DIM
    outs = []
    for c in range(N_SLABS):
        blk = t[:, c * LANES:(c + 1) * LANES]
        sq = blk * blk
        s_lo = jnp.sum(jnp.where(lo, sq, 0.0), axis=-1, keepdims=True)
        s_hi = jnp.sum(jnp.where(lo, 0.0, sq), axis=-1, keepdims=True)
        ms = jnp.where(lo, s_lo, s_hi) * (1.0 / SB_HEAD_DIM)
        outs.append(blk * lax.rsqrt(ms + EPS))
    return jnp.concatenate(outs, axis=-1)


def _layer0_kernel(x_ref, p_ref, a_norm_ref, a_scale_ref, kv_norm_ref, b_norm_ref, k_gain_ref, q_gain_ref,
                   w_in_ref, w_group_ref, w_out_ref, ple_w_ref, gate_w_ref, w_kv_ref, bw_in_ref,
                   x1_ref, q_ref, k_ref, v_ref, z_ref, halo_ref):
    t = pl.program_id(1)
    rows = x_ref.shape[1]

    @pl.when(t == 0)
    def _():
        halo_ref[...] = jnp.zeros_like(halo_ref)

    x = x_ref[0]
    h = x * _rms_scale(x) * a_norm_ref[...]
    uz = jnp.dot(h.astype(BF16), w_in_ref[...], preferred_element_type=F32)
    u = uz[:, :D_MODEL]
    z = uz[:, D_MODEL:]

    ext = jnp.concatenate([halo_ref[...], u], axis=0)
    halo_ref[...] = u[rows - POOL_HALO:, :]
    pos1 = t * rows + lax.broadcasted_iota(jnp.int32, (rows, 1), 0) + 1
    mixed = []
    for g, w in enumerate(POOL_WINDOWS):
        e = ext[:, g * POOL_GROUP_DIM:(g + 1) * POOL_GROUP_DIM]
        s = e
        shift = 1
        while shift < w:
            s = s + pltpu.roll(s, shift, axis=0)
            shift *= 2
        inv_cnt = 1.0 / jnp.minimum(pos1, w).astype(F32)
        pooled = s[POOL_HALO:, :] * inv_cnt - e[POOL_HALO:, :]
        mixed.append(jnp.dot(pooled.astype(BF16), w_group_ref[g], preferred_element_type=F32))
    mixed = jnp.concatenate(mixed, axis=-1) * a_scale_ref[...]
    gated = mixed * (z * _sigmoid(z))
    x = x + jnp.dot(gated.astype(BF16), w_out_ref[...], preferred_element_type=F32)

    pe = jnp.dot(p_ref[0, 0].astype(BF16), ple_w_ref[...], preferred_element_type=F32)
    gate = _sigmoid(jnp.dot(x.astype(BF16), gate_w_ref[...], preferred_element_type=F32))
    x = x + pe * gate
    x1_ref[0] = x

    xn = x * _rms_scale(x)
    kv = jnp.dot((xn * kv_norm_ref[...]).astype(BF16), w_kv_ref[...], preferred_element_type=F32)
    qz = jnp.dot((xn * b_norm_ref[...]).astype(BF16), bw_in_ref[...], preferred_element_type=F32)
    k = (_head_rms(kv[:, :D_MODEL]) * k_gain_ref[...]).astype(BF16)
    v = kv[:, D_MODEL:].astype(BF16)
    q = (_head_rms(qz[:, :D_MODEL]) * q_gain_ref[...] * (SB_SCALE * LOG2_E)).astype(BF16)
    z_ref[0] = qz[:, D_MODEL:]
    for c in range(N_SLABS):
        sl = slice(c * LANES, (c + 1) * LANES)
        q_ref[0, c] = q[:, sl]
        k_ref[0, c] = k[:, sl]
        v_ref[0, c] = v[:, sl]


def _attn_kernel(q_ref, k_ref, v_ref, z_ref, x1_ref, p_ref, tri_ref, w_out_ref, ple_w_ref, gate_w_ref,
                 out_ref, o_scr, c_scr, qm_scr, lg_a, sp_a, lg_b, sp_b):
    i = pl.program_id(1)
    tq = q_ref.shape[2]
    lo = lax.broadcasted_iota(jnp.int32, (1, LANES), 1) < SB_HEAD_DIM
    row = lax.broadcasted_iota(jnp.int32, (tq, K_TILE), 0)
    col = lax.broadcasted_iota(jnp.int32, (tq, K_TILE), 1)
    causal = col < row
    tri = tri_ref[...]
    buf_a = (lg_a, sp_a)
    buf_b = (lg_b, sp_b)

    o_scr[...] = jnp.zeros_like(o_scr)
    c_scr[...] = jnp.zeros_like(c_scr)
    for slab in range(N_SLABS):
        q2 = q_ref[0, slab]
        zero = jnp.zeros_like(q2)
        qm_scr[HEADS_PER_SLAB * slab] = jnp.where(lo, q2, zero)
        qm_scr[HEADS_PER_SLAB * slab + 1] = jnp.where(lo, zero, q2)

    def item_coords(n):
        if isinstance(n, int):
            group, back = n % GROUPS, n // GROUPS
        else:
            group, back = lax.rem(n, GROUPS), lax.div(n, GROUPS)
        start = pl.multiple_of(jnp.maximum(i - back, 0) * K_TILE, K_TILE)
        return [SLABS_PER_ITEM * group + s for s in range(SLABS_PER_ITEM)], start

    def qk_logits(n, buf):
        slabs, start = item_coords(n)
        for s, slab in enumerate(slabs):
            k2 = k_ref[0, slab, pl.ds(start, K_TILE), :]
            for hh in range(HEADS_PER_SLAB):
                buf[0][HEADS_PER_SLAB * s + hh] = lax.dot_general(
                    qm_scr[HEADS_PER_SLAB * slab + hh], k2, (((1,), (1,)), ((), ())), preferred_element_type=F32)

    def stage2(buf, diagonal):
        lg_buf, sp_buf = buf
        for h in range(HEADS_PER_ITEM):
            l = lg_buf[h]
            sp = jnp.maximum(l, 0.0) + jnp.log2(1.0 + jnp.exp2(-jnp.abs(l)))
            if diagonal:
                sp = jnp.where(causal, sp, 0.0)
                lg_buf[h] = jnp.where(causal, l, MASKED_LOG2)
            sp_buf[h] = sp.astype(BF16)

    def suffix_sums(buf):
        return [jnp.dot(buf[1][h], tri, preferred_element_type=F32) for h in range(HEADS_PER_ITEM)]

    def weights(incl, buf):
        return [jnp.exp2(buf[0][h] - incl[h]).astype(BF16) for h in range(HEADS_PER_ITEM)]

    def accumulate(n, a, incl):
        slabs, start = item_coords(n)
        for s, slab in enumerate(slabs):
            v2 = v_ref[0, slab, pl.ds(start, K_TILE), :]
            zero = jnp.zeros_like(v2)
            pv = (jnp.dot(a[HEADS_PER_SLAB * s], jnp.where(lo, v2, zero), preferred_element_type=F32)
                  + jnp.dot(a[HEADS_PER_SLAB * s + 1], jnp.where(lo, zero, v2), preferred_element_type=F32))
            c = c_scr[slab]
            o_scr[slab] = o_scr[slab] + pv * jnp.exp2(-c)
            c_scr[slab] = c + jnp.where(lo, incl[HEADS_PER_SLAB * s][:, 0:1], incl[HEADS_PER_SLAB * s + 1][:, 0:1])

    def step(m, cur, nxt, diagonal):
        incl = suffix_sums(cur)
        a = weights(incl, cur)
        qk_logits(m + 2, cur)
        stage2(nxt, diagonal)
        accumulate(m, a, incl)

    qk_logits(0, buf_a)
    qk_logits(1, buf_b)
    stage2(buf_a, True)
    for m in range(GROUPS - 1):
        cur, nxt = (buf_a, buf_b) if m % 2 == 0 else (buf_b, buf_a)
        step(m, cur, nxt, True)

    def step_pair(t, _):
        m = GROUPS - 1 + 2 * t
        step(m, buf_b, buf_a, False)
        step(m + 1, buf_a, buf_b, False)
        return 0

    lax.fori_loop(0, (GROUPS // 2) * i, step_pair, 0)
    last = GROUPS * i + GROUPS - 1
    incl = suffix_sums(buf_b)
    accumulate(last, weights(incl, buf_b), incl)

    o = jnp.concatenate([o_scr[c] for c in range(N_SLABS)], axis=-1)
    z = z_ref[0]
    gated = o * (z * _sigmoid(z))
    x = x1_ref[0] + jnp.dot(gated.astype(BF16), w_out_ref[...], preferred_element_type=F32)
    pe = jnp.dot(p_ref[0, 0].astype(BF16), ple_w_ref[...], preferred_element_type=F32)
    gate = _sigmoid(jnp.dot(x.astype(BF16), gate_w_ref[...], preferred_element_type=F32))
    out_ref[0] = x + pe * gate


def _resident(shape):
    return pl.BlockSpec(shape, lambda b, t: (0,) * len(shape), pipeline_mode=pl.Buffered(1))


def kernel(x, p, a_norm, a_w_in, a_w_group, a_scale, a_w_out, kv_norm, w_kv, k_norm, b_norm, b_w_in, b_q_norm,
           b_w_out, ple_w, ple_gate_w):
    B, S, D = x.shape
    assert D == D_MODEL and S % ROW_TILE == 0 and S % Q_TILE == 0 and Q_TILE == K_TILE
    assert GROUPS % 2 == 0
    assert a_norm.shape[0] == 1 and b_norm.shape[0] == 1 and p.shape[0] == 2

    row = lambda a: a.reshape(1, D_MODEL).astype(F32)
    per_head = lambda a: jnp.tile(a.astype(F32), SB_HEADS).reshape(1, D_MODEL)
    wb = lambda a: a.astype(BF16)

    slab_shape = jax.ShapeDtypeStruct((B, N_SLABS, S, LANES), BF16)
    tok_spec = pl.BlockSpec((1, ROW_TILE, D_MODEL), lambda b, t: (b, t, 0))
    slab_spec = pl.BlockSpec((1, N_SLABS, ROW_TILE, LANES), lambda b, t: (b, 0, t, 0))
    vec_spec = _resident((1, D_MODEL))

    x1, q, k, v, z = pl.pallas_call(
        _layer0_kernel,
        grid=(B, S // ROW_TILE),
        in_specs=[
            tok_spec,
            pl.BlockSpec((1, 1, ROW_TILE, PLE_DIM), lambda b, t: (0, b, t, 0)),
            vec_spec, vec_spec, vec_spec, vec_spec, vec_spec, vec_spec,
            _resident((D_MODEL, 2 * D_MODEL)),
            _resident((len(POOL_WINDOWS), POOL_GROUP_DIM, POOL_GROUP_DIM)),
            _resident((D_MODEL, D_MODEL)),
            _resident((PLE_DIM, D_MODEL)),
            _resident((D_MODEL, D_MODEL)),
            _resident((D_MODEL, 2 * D_MODEL)),
            _resident((D_MODEL, 2 * D_MODEL)),
        ],
        out_specs=[tok_spec, slab_spec, slab_spec, slab_spec, tok_spec],
        out_shape=[jax.ShapeDtypeStruct((B, S, D_MODEL), F32), slab_shape, slab_shape, slab_shape,
                   jax.ShapeDtypeStruct((B, S, D_MODEL), F32)],
        scratch_shapes=[pltpu.VMEM((POOL_HALO, D_MODEL), F32)],
        compiler_params=pltpu.CompilerParams(
            dimension_semantics=("arbitrary", "arbitrary"), vmem_limit_bytes=VMEM_LIMIT_BYTES),
        name="layer0_and_projections",
    )(x, p, row(a_norm[0]), row(a_scale[0]), row(kv_norm), row(b_norm[0]), per_head(k_norm), per_head(b_q_norm[0]),
      wb(a_w_in[0]), wb(a_w_group[0]), wb(a_w_out[0]), wb(ple_w[0]), wb(ple_gate_w[0]), wb(w_kv), wb(b_w_in[0]))

    tri = (lax.broadcasted_iota(jnp.int32, (K_TILE, K_TILE), 0)
           >= lax.broadcasted_iota(jnp.int32, (K_TILE, K_TILE), 1)).astype(BF16)
    qtok_spec = pl.BlockSpec((1, Q_TILE, D_MODEL), lambda b, i: (b, i, 0))
    seq_spec = pl.BlockSpec((1, N_SLABS, S, LANES), lambda b, i: (b, 0, 0, 0))
    stage_buffers = [pltpu.VMEM((HEADS_PER_ITEM, Q_TILE, K_TILE), F32),
                     pltpu.VMEM((HEADS_PER_ITEM, Q_TILE, K_TILE), BF16)]

    return pl.pallas_call(
        _attn_kernel,
        grid=(B, S // Q_TILE),
        in_specs=[
            pl.BlockSpec((1, N_SLABS, Q_TILE, LANES), lambda b, i: (b, 0, i, 0)),
            seq_spec, seq_spec,
            qtok_spec, qtok_spec,
            pl.BlockSpec((1, 1, Q_TILE, PLE_DIM), lambda b, i: (1, b, i, 0)),
            _resident((K_TILE, K_TILE)),
            _resident((D_MODEL, D_MODEL)),
            _resident((PLE_DIM, D_MODEL)),
            _resident((D_MODEL, D_MODEL)),
        ],
        out_specs=qtok_spec,
        out_shape=jax.ShapeDtypeStruct((B, S, D_MODEL), F32),
        scratch_shapes=[pltpu.VMEM((N_SLABS, Q_TILE, LANES), F32),
                        pltpu.VMEM((N_SLABS, Q_TILE, LANES), F32),
                        pltpu.VMEM((SB_HEADS, Q_TILE, LANES), BF16)]
                       + stage_buffers + stage_buffers,
        compiler_params=pltpu.CompilerParams(
            dimension_semantics=("arbitrary", "arbitrary"), vmem_limit_bytes=VMEM_LIMIT_BYTES),
        name="stick_breaking_layer",
    )(q, k, v, z, x1, p, tri, wb(b_w_out[0]), wb(ple_w[1]), wb(ple_gate_w[1]))
```

```python
import jax
import jax.numpy as jnp
from jax import lax
from jax.experimental import pallas as pl
from jax.experimental.pallas import tpu as pltpu

D_MODEL = 1024
PLE_DIM = 256
POOL_WINDOWS = (2, 4, 8, 16)
POOL_GROUP_DIM = D_MODEL // len(POOL_WINDOWS)
POOL_HALO = 16
SB_HEADS = 16
SB_HEAD_DIM = D_MODEL // SB_HEADS
SB_SCALE = SB_HEAD_DIM ** -0.5
EPS = 1e-6
LOG2_E = 1.4426950408889634

LANES = 128
HEADS_PER_SLAB = LANES // SB_HEAD_DIM
N_SLABS = D_MODEL // LANES

ROW_TILE = 512
Q_TILE = 256
K_TILE = 256
SLABS_PER_ITEM = 1
HEADS_PER_ITEM = SLABS_PER_ITEM * HEADS_PER_SLAB
GROUPS = N_SLABS // SLABS_PER_ITEM
MASKED_LOG2 = -1e30
VMEM_LIMIT_BYTES = 56 * 1024 * 1024

F32 = jnp.float32
BF16 = jnp.bfloat16


def _sigmoid(t):
    return 1.0 / (1.0 + jnp.exp(-t))


def _rms_scale(t):
    return lax.rsqrt(jnp.mean(t * t, axis=-1, keepdims=True) + EPS)


def _head_rms(t):
    rows = t.shape[0]
    lo = lax.broadcasted_iota(jnp.int32, (rows, LANES), 1) < SB_HEAD_DIM
    outs = []
    for c in range(N_SLABS):
        blk = t[:, c * LANES:(c + 1) * LANES]
        sq = blk * blk
        s_lo = jnp.sum(jnp.where(lo, sq, 0.0), axis=-1, keepdims=True)
        s_hi = jnp.sum(jnp.where(lo, 0.0, sq), axis=-1, keepdims=True)
        ms = jnp.where(lo, s_lo, s_hi) * (1.0 / SB_HEAD_DIM)
        outs.append(blk * lax.rsqrt(ms + EPS))
    return jnp.concatenate(outs, axis=-1)


def _layer0_kernel(x_ref, p_ref, a_norm_ref, a_scale_ref, kv_norm_ref, b_norm_ref, k_gain_ref, q_gain_ref,
                   w_in_ref, w_group_ref, w_out_ref, ple_w_ref, gate_w_ref, w_kv_ref, bw_in_ref,
                   x1_ref, q_ref, k_ref, v_ref, z_ref, halo_ref):
    t = pl.program_id(1)
    rows = x_ref.shape[1]

    @pl.when(t == 0)
    def _():
        halo_ref[...] = jnp.zeros_like(halo_ref)

    x = x_ref[0]
    h = x * _rms_scale(x) * a_norm_ref[...]
    uz = jnp.dot(h.astype(BF16), w_in_ref[...], preferred_element_type=F32)
    u = uz[:, :D_MODEL]
    z = uz[:, D_MODEL:]

    ext = jnp.concatenate([halo_ref[...], u], axis=0)
    halo_ref[...] = u[rows - POOL_HALO:, :]
    pos1 = t * rows + lax.broadcasted_iota(jnp.int32, (rows, 1), 0) + 1
    mixed = []
    for g, w in enumerate(POOL_WINDOWS):
        e = ext[:, g * POOL_GROUP_DIM:(g + 1) * POOL_GROUP_DIM]
        s = e
        shift = 1
        while shift < w:
            s = s + pltpu.roll(s, shift, axis=0)
            shift *= 2
        inv_cnt = 1.0 / jnp.minimum(pos1, w).astype(F32)
        pooled = s[POOL_HALO:, :] * inv_cnt - e[POOL_HALO:, :]
        mixed.append(jnp.dot(pooled.astype(BF16), w_group_ref[g], preferred_element_type=F32))
    mixed = jnp.concatenate(mixed, axis=-1) * a_scale_ref[...]
    gated = mixed * (z * _sigmoid(z))
    x = x + jnp.dot(gated.astype(BF16), w_out_ref[...], preferred_element_type=F32)

    pe = jnp.dot(p_ref[0, 0].astype(BF16), ple_w_ref[...], preferred_element_type=F32)
    gate = _sigmoid(jnp.dot(x.astype(BF16), gate_w_ref[...], preferred_element_type=F32))
    x = x + pe * gate
    x1_ref[0] = x

    xn = x * _rms_scale(x)
    kv = jnp.dot((xn * kv_norm_ref[...]).astype(BF16), w_kv_ref[...], preferred_element_type=F32)
    qz = jnp.dot((xn * b_norm_ref[...]).astype(BF16), bw_in_ref[...], preferred_element_type=F32)
    k = (_head_rms(kv[:, :D_MODEL]) * k_gain_ref[...]).astype(BF16)
    v = kv[:, D_MODEL:].astype(BF16)
    q = (_head_rms(qz[:, :D_MODEL]) * q_gain_ref[...] * (SB_SCALE * LOG2_E)).astype(BF16)
    z_ref[0] = qz[:, D_MODEL:]
    for c in range(N_SLABS):
        sl = slice(c * LANES, (c + 1) * LANES)
        q_ref[0, c] = q[:, sl]
        k_ref[0, c] = k[:, sl]
        v_ref[0, c] = v[:, sl]


def _attn_kernel(q_ref, k_ref, v_ref, z_ref, x1_ref, p_ref, tri_ref, w_out_ref, ple_w_ref, gate_w_ref,
                 out_ref, o_scr, c_scr, qm_scr, lg_a, sp_a, a_a, dec_a, lg_b, sp_b, a_b, dec_b):
    i = pl.program_id(1)
    tq = q_ref.shape[2]
    lo = lax.broadcasted_iota(jnp.int32, (1, LANES), 1) < SB_HEAD_DIM
    row = lax.broadcasted_iota(jnp.int32, (tq, K_TILE), 0)
    col = lax.broadcasted_iota(jnp.int32, (tq, K_TILE), 1)
    causal = col < row
    tri = tri_ref[...]
    buf_a = (lg_a, sp_a, a_a, dec_a)
    buf_b = (lg_b, sp_b, a_b, dec_b)

    o_scr[...] = jnp.zeros_like(o_scr)
    c_scr[...] = jnp.zeros_like(c_scr)
    for slab in range(N_SLABS):
        q2 = q_ref[0, slab]
        zero = jnp.zeros_like(q2)
        qm_scr[HEADS_PER_SLAB * slab] = jnp.where(lo, q2, zero)
        qm_scr[HEADS_PER_SLAB * slab + 1] = jnp.where(lo, zero, q2)

    def item_coords(n):
        if isinstance(n, int):
            group, back = n % GROUPS, n // GROUPS
        else:
            group, back = lax.rem(n, GROUPS), lax.div(n, GROUPS)
        start = pl.multiple_of(jnp.maximum(i - back, 0) * K_TILE, K_TILE)
        return [SLABS_PER_ITEM * group + s for s in range(SLABS_PER_ITEM)], start

    def qk_logits(n, buf):
        slabs, start = item_coords(n)
        for s, slab in enumerate(slabs):
            k2 = k_ref[0, slab, pl.ds(start, K_TILE), :]
            for hh in range(HEADS_PER_SLAB):
                buf[0][HEADS_PER_SLAB * s + hh] = lax.dot_general(
                    qm_scr[HEADS_PER_SLAB * slab + hh], k2, (((1,), (1,)), ((), ())), preferred_element_type=F32)

    def stage2(buf, diagonal):
        lg_buf, sp_buf = buf[0], buf[1]
        for h in range(HEADS_PER_ITEM):
            l = lg_buf[h]
            sp = jnp.maximum(l, 0.0) + jnp.log2(1.0 + jnp.exp2(-jnp.abs(l)))
            if diagonal:
                sp = jnp.where(causal, sp, 0.0)
                lg_buf[h] = jnp.where(causal, l, MASKED_LOG2)
            sp_buf[h] = sp.astype(BF16)

    def suffix_sums(buf):
        return [jnp.dot(buf[1][h], tri, preferred_element_type=F32) for h in range(HEADS_PER_ITEM)]

    def stage3(n, buf, incl):
        lg_buf, _, a_buf, dec_buf = buf
        for h in range(HEADS_PER_ITEM):
            a_buf[h] = jnp.exp2(lg_buf[h] - incl[h]).astype(BF16)
        slabs, _ = item_coords(n)
        for s, slab in enumerate(slabs):
            c = c_scr[slab]
            dec_buf[s] = jnp.exp2(-c)
            c_scr[slab] = c + jnp.where(lo, incl[HEADS_PER_SLAB * s][:, 0:1], incl[HEADS_PER_SLAB * s + 1][:, 0:1])

    def pv_dots(n, buf):
        slabs, start = item_coords(n)
        pvs = []
        for s, slab in enumerate(slabs):
            v2 = v_ref[0, slab, pl.ds(start, K_TILE), :]
            zero = jnp.zeros_like(v2)
            pvs.append(jnp.dot(buf[2][HEADS_PER_SLAB * s], jnp.where(lo, v2, zero), preferred_element_type=F32)
                       + jnp.dot(buf[2][HEADS_PER_SLAB * s + 1], jnp.where(lo, zero, v2),
                                 preferred_element_type=F32))
        return pvs

    def stage4(n, buf, pvs):
        slabs, _ = item_coords(n)
        for s, slab in enumerate(slabs):
            o_scr[slab] = o_scr[slab] + pvs[s] * buf[3][s]

    def step(m, cur, nxt, diagonal):
        pvs = pv_dots(m, cur)
        incl = suffix_sums(nxt)
        stage3(m + 1, nxt, incl)
        qk_logits(m + 3, nxt)
        stage2(cur, diagonal)
        stage4(m, cur, pvs)

    qk_logits(0, buf_a)
    qk_logits(1, buf_b)
    stage2(buf_a, True)
    stage3(0, buf_a, suffix_sums(buf_a))
    qk_logits(2, buf_a)
    stage2(buf_b, True)
    for m in range(GROUPS - 1):
        cur, nxt = (buf_a, buf_b) if m % 2 == 0 else (buf_b, buf_a)
        step(m, cur, nxt, m + 2 < GROUPS)

    def step_group(t, _):
        m = GROUPS - 1 + GROUPS * t
        for r in range(0, GROUPS, 2):
            step(m + r, buf_b, buf_a, False)
            step(m + r + 1, buf_a, buf_b, False)
        return 0

    lax.fori_loop(0, i, step_group, 0)
    last = GROUPS * i + GROUPS - 1
    stage4(last, buf_b, pv_dots(last, buf_b))

    o = jnp.concatenate([o_scr[c] for c in range(N_SLABS)], axis=-1)
    z = z_ref[0]
    gated = o * (z * _sigmoid(z))
    x = x1_ref[0] + jnp.dot(gated.astype(BF16), w_out_ref[...], preferred_element_type=F32)
    pe = jnp.dot(p_ref[0, 0].astype(BF16), ple_w_ref[...], preferred_element_type=F32)
    gate = _sigmoid(jnp.dot(x.astype(BF16), gate_w_ref[...], preferred_element_type=F32))
    out_ref[0] = x + pe * gate


def _resident(shape):
    return pl.BlockSpec(shape, lambda b, t: (0,) * len(shape), pipeline_mode=pl.Buffered(1))


def kernel(x, p, a_norm, a_w_in, a_w_group, a_scale, a_w_out, kv_norm, w_kv, k_norm, b_norm, b_w_in, b_q_norm,
           b_w_out, ple_w, ple_gate_w):
    B, S, D = x.shape
    assert D == D_MODEL and S % ROW_TILE == 0 and S % Q_TILE == 0 and Q_TILE == K_TILE
    assert GROUPS % 2 == 0
    assert a_norm.shape[0] == 1 and b_norm.shape[0] == 1 and p.shape[0] == 2

    row = lambda a: a.reshape(1, D_MODEL).astype(F32)
    per_head = lambda a: jnp.tile(a.astype(F32), SB_HEADS).reshape(1, D_MODEL)
    wb = lambda a: a.astype(BF16)

    slab_shape = jax.ShapeDtypeStruct((B, N_SLABS, S, LANES), BF16)
    tok_spec = pl.BlockSpec((1, ROW_TILE, D_MODEL), lambda b, t: (b, t, 0))
    slab_spec = pl.BlockSpec((1, N_SLABS, ROW_TILE, LANES), lambda b, t: (b, 0, t, 0))
    vec_spec = _resident((1, D_MODEL))

    x1, q, k, v, z = pl.pallas_call(
        _layer0_kernel,
        grid=(B, S // ROW_TILE),
        in_specs=[
            tok_spec,
            pl.BlockSpec((1, 1, ROW_TILE, PLE_DIM), lambda b, t: (0, b, t, 0)),
            vec_spec, vec_spec, vec_spec, vec_spec, vec_spec, vec_spec,
            _resident((D_MODEL, 2 * D_MODEL)),
            _resident((len(POOL_WINDOWS), POOL_GROUP_DIM, POOL_GROUP_DIM)),
            _resident((D_MODEL, D_MODEL)),
            _resident((PLE_DIM, D_MODEL)),
            _resident((D_MODEL, D_MODEL)),
            _resident((D_MODEL, 2 * D_MODEL)),
            _resident((D_MODEL, 2 * D_MODEL)),
        ],
        out_specs=[tok_spec, slab_spec, slab_spec, slab_spec, tok_spec],
        out_shape=[jax.ShapeDtypeStruct((B, S, D_MODEL), F32), slab_shape, slab_shape, slab_shape,
                   jax.ShapeDtypeStruct((B, S, D_MODEL), F32)],
        scratch_shapes=[pltpu.VMEM((POOL_HALO, D_MODEL), F32)],
        compiler_params=pltpu.CompilerParams(
            dimension_semantics=("arbitrary", "arbitrary"), vmem_limit_bytes=VMEM_LIMIT_BYTES),
        name="layer0_and_projections",
    )(x, p, row(a_norm[0]), row(a_scale[0]), row(kv_norm), row(b_norm[0]), per_head(k_norm), per_head(b_q_norm[0]),
      wb(a_w_in[0]), wb(a_w_group[0]), wb(a_w_out[0]), wb(ple_w[0]), wb(ple_gate_w[0]), wb(w_kv), wb(b_w_in[0]))

    tri = (lax.broadcasted_iota(jnp.int32, (K_TILE, K_TILE), 0)
           >= lax.broadcasted_iota(jnp.int32, (K_TILE, K_TILE), 1)).astype(BF16)
    qtok_spec = pl.BlockSpec((1, Q_TILE, D_MODEL), lambda b, i: (b, i, 0))
    seq_spec = pl.BlockSpec((1, N_SLABS, S, LANES), lambda b, i: (b, 0, 0, 0))
    stage_buffers = [pltpu.VMEM((HEADS_PER_ITEM, Q_TILE, K_TILE), F32),
                     pltpu.VMEM((HEADS_PER_ITEM, Q_TILE, K_TILE), BF16),
                     pltpu.VMEM((HEADS_PER_ITEM, Q_TILE, K_TILE), BF16),
                     pltpu.VMEM((SLABS_PER_ITEM, Q_TILE, LANES), F32)]

    return pl.pallas_call(
        _attn_kernel,
        grid=(B, S // Q_TILE),
        in_specs=[
            pl.BlockSpec((1, N_SLABS, Q_TILE, LANES), lambda b, i: (b, 0, i, 0)),
            seq_spec, seq_spec,
            qtok_spec, qtok_spec,
            pl.BlockSpec((1, 1, Q_TILE, PLE_DIM), lambda b, i: (1, b, i, 0)),
            _resident((K_TILE, K_TILE)),
            _resident((D_MODEL, D_MODEL)),
            _resident((PLE_DIM, D_MODEL)),
            _resident((D_MODEL, D_MODEL)),
        ],
        out_specs=qtok_spec,
        out_shape=jax.ShapeDtypeStruct((B, S, D_MODEL), F32),
        scratch_shapes=[pltpu.VMEM((N_SLABS, Q_TILE, LANES), F32),
                        pltpu.VMEM((N_SLABS, Q_TILE, LANES), F32),
                        pltpu.VMEM((SB_HEADS, Q_TILE, LANES), BF16)]
                       + stage_buffers + stage_buffers,
        compiler_params=pltpu.CompilerParams(
            dimension_semantics=("arbitrary", "arbitrary"), vmem_limit_bytes=VMEM_LIMIT_BYTES),
        name="stick_breaking_layer",
    )(q, k, v, z, x1, p, tri, wb(b_w_out[0]), wb(ple_w[1]), wb(ple_gate_w[1]))
```

```python
import jax
import jax.numpy as jnp
from jax import lax
from jax.experimental import pallas as pl
from jax.experimental.pallas import tpu as pltpu

D_MODEL = 1024
PLE_DIM = 256
POOL_WINDOWS = (2, 4, 8, 16)
POOL_GROUP_DIM = D_MODEL // len(POOL_WINDOWS)
POOL_HALO = 16
SB_HEADS = 16
SB_HEAD_DIM = D_MODEL // SB_HEADS
SB_SCALE = SB_HEAD_DIM ** -0.5
EPS = 1e-6
LOG2_E = 1.4426950408889634

LANES = 128
HEADS_PER_SLAB = LANES // SB_HEAD_DIM
N_SLABS = D_MODEL // LANES

ROW_TILE = 512
Q_TILE = 256
K_TILE = 256
SLABS_PER_ITEM = 1
HEADS_PER_ITEM = SLABS_PER_ITEM * HEADS_PER_SLAB
GROUPS = N_SLABS // SLABS_PER_ITEM
TILES_PER_EPILOGUE = 2
MASKED_LOG2 = -1e30
VMEM_LIMIT_BYTES = 56 * 1024 * 1024

F32 = jnp.float32
BF16 = jnp.bfloat16


def _sigmoid(t):
    return 1.0 / (1.0 + jnp.exp(-t))


def _rms_scale(t):
    return lax.rsqrt(jnp.mean(t * t, axis=-1, keepdims=True) + EPS)


def _head_rms(t):
    rows = t.shape[0]
    lo = lax.broadcasted_iota(jnp.int32, (rows, LANES), 1) < SB_HEAD_DIM
    outs = []
    for c in range(N_SLABS):
        blk = t[:, c * LANES:(c + 1) * LANES]
        sq = blk * blk
        s_lo = jnp.sum(jnp.where(lo, sq, 0.0), axis=-1, keepdims=True)
        s_hi = jnp.sum(jnp.where(lo, 0.0, sq), axis=-1, keepdims=True)
        ms = jnp.where(lo, s_lo, s_hi) * (1.0 / SB_HEAD_DIM)
        outs.append(blk * lax.rsqrt(ms + EPS))
    return jnp.concatenate(outs, axis=-1)


def _layer0_kernel(x_ref, p_ref, a_norm_ref, a_scale_ref, kv_norm_ref, b_norm_ref, k_gain_ref, q_gain_ref,
                   w_in_ref, w_group_ref, w_out_ref, ple_w_ref, gate_w_ref, w_kv_ref, bw_in_ref,
                   x1_ref, q_ref, k_ref, v_ref, z_ref, halo_ref):
    t = pl.program_id(1)
    rows = x_ref.shape[1]

    @pl.when(t == 0)
    def _():
        halo_ref[...] = jnp.zeros_like(halo_ref)

    x = x_ref[0]
    h = x * _rms_scale(x) * a_norm_ref[...]
    uz = jnp.dot(h.astype(BF16), w_in_ref[...], preferred_element_type=F32)
    u = uz[:, :D_MODEL]
    z = uz[:, D_MODEL:]

    ext = jnp.concatenate([halo_ref[...], u], axis=0)
    halo_ref[...] = u[rows - POOL_HALO:, :]
    pos1 = t * rows + lax.broadcasted_iota(jnp.int32, (rows, 1), 0) + 1
    mixed = []
    for g, w in enumerate(POOL_WINDOWS):
        e = ext[:, g * POOL_GROUP_DIM:(g + 1) * POOL_GROUP_DIM]
        s = e
        shift = 1
        while shift < w:
            s = s + pltpu.roll(s, shift, axis=0)
            shift *= 2
        inv_cnt = 1.0 / jnp.minimum(pos1, w).astype(F32)
        pooled = s[POOL_HALO:, :] * inv_cnt - e[POOL_HALO:, :]
        mixed.append(jnp.dot(pooled.astype(BF16), w_group_ref[g], preferred_element_type=F32))
    mixed = jnp.concatenate(mixed, axis=-1) * a_scale_ref[...]
    gated = mixed * (z * _sigmoid(z))
    x = x + jnp.dot(gated.astype(BF16), w_out_ref[...], preferred_element_type=F32)

    pe = jnp.dot(p_ref[0, 0].astype(BF16), ple_w_ref[...], preferred_element_type=F32)
    gate = _sigmoid(jnp.dot(x.astype(BF16), gate_w_ref[...], preferred_element_type=F32))
    x = x + pe * gate
    x1_ref[0] = x

    xn = x * _rms_scale(x)
    kv = jnp.dot((xn * kv_norm_ref[...]).astype(BF16), w_kv_ref[...], preferred_element_type=F32)
    qz = jnp.dot((xn * b_norm_ref[...]).astype(BF16), bw_in_ref[...], preferred_element_type=F32)
    k = (_head_rms(kv[:, :D_MODEL]) * k_gain_ref[...]).astype(BF16)
    v = kv[:, D_MODEL:].astype(BF16)
    q = (_head_rms(qz[:, :D_MODEL]) * q_gain_ref[...] * (SB_SCALE * LOG2_E)).astype(BF16)
    z_ref[0] = qz[:, D_MODEL:]
    for c in range(N_SLABS):
        sl = slice(c * LANES, (c + 1) * LANES)
        q_ref[0, c] = q[:, sl]
        k_ref[0, c] = k[:, sl]
        v_ref[0, c] = v[:, sl]


def _attn_kernel(q_ref, k_ref, v_ref, z_ref, x1_ref, p_ref, tri_ref, w_out_ref, ple_w_ref, gate_w_ref,
                 out_ref, o_scr, c_scr, qm_scr, lg_a, sp_a, a_a, dec_a, lg_b, sp_b, a_b, dec_b):
    i = pl.program_id(1)
    tq = q_ref.shape[2]
    lo = lax.broadcasted_iota(jnp.int32, (1, LANES), 1) < SB_HEAD_DIM
    row = lax.broadcasted_iota(jnp.int32, (tq, K_TILE), 0)
    col = lax.broadcasted_iota(jnp.int32, (tq, K_TILE), 1)
    causal = col < row
    tri = tri_ref[...]
    buf_a = (lg_a, sp_a, a_a, dec_a)
    buf_b = (lg_b, sp_b, a_b, dec_b)

    o_base = lax.rem(i, TILES_PER_EPILOGUE) * N_SLABS
    o_scr[pl.ds(o_base, N_SLABS)] = jnp.zeros((N_SLABS,) + o_scr.shape[1:], F32)
    c_scr[...] = jnp.zeros_like(c_scr)
    for slab in range(N_SLABS):
        q2 = q_ref[0, slab]
        zero = jnp.zeros_like(q2)
        qm_scr[HEADS_PER_SLAB * slab] = jnp.where(lo, q2, zero)
        qm_scr[HEADS_PER_SLAB * slab + 1] = jnp.where(lo, zero, q2)

    def item_coords(n):
        if isinstance(n, int):
            group, back = n % GROUPS, n // GROUPS
        else:
            group, back = lax.rem(n, GROUPS), lax.div(n, GROUPS)
        start = pl.multiple_of(jnp.maximum(i - back, 0) * K_TILE, K_TILE)
        return [SLABS_PER_ITEM * group + s for s in range(SLABS_PER_ITEM)], start

    def qk_logits(n, buf):
        slabs, start = item_coords(n)
        for s, slab in enumerate(slabs):
            k2 = k_ref[0, slab, pl.ds(start, K_TILE), :]
            for hh in range(HEADS_PER_SLAB):
                buf[0][HEADS_PER_SLAB * s + hh] = lax.dot_general(
                    qm_scr[HEADS_PER_SLAB * slab + hh], k2, (((1,), (1,)), ((), ())), preferred_element_type=F32)

    def stage2(buf, diagonal):
        lg_buf, sp_buf = buf[0], buf[1]
        for h in range(HEADS_PER_ITEM):
            l = lg_buf[h]
            sp = jnp.maximum(l, 0.0) + jnp.log2(1.0 + jnp.exp2(-jnp.abs(l)))
            if diagonal:
                sp = jnp.where(causal, sp, 0.0)
                lg_buf[h] = jnp.where(causal, l, MASKED_LOG2)
            sp_buf[h] = sp.astype(BF16)

    def suffix_sums(buf):
        return [jnp.dot(buf[1][h], tri, preferred_element_type=F32) for h in range(HEADS_PER_ITEM)]

    def stage3(n, buf, incl):
        lg_buf, _, a_buf, dec_buf = buf
        for h in range(HEADS_PER_ITEM):
            a_buf[h] = jnp.exp2(lg_buf[h] - incl[h]).astype(BF16)
        slabs, _ = item_coords(n)
        for s, slab in enumerate(slabs):
            c = c_scr[slab]
            dec_buf[s] = jnp.exp2(-c)
            c_scr[slab] = c + jnp.where(lo, incl[HEADS_PER_SLAB * s][:, 0:1], incl[HEADS_PER_SLAB * s + 1][:, 0:1])

    def pv_dots(n, buf):
        slabs, start = item_coords(n)
        pvs = []
        for s, slab in enumerate(slabs):
            v2 = v_ref[0, slab, pl.ds(start, K_TILE), :]
            zero = jnp.zeros_like(v2)
            pvs.append(jnp.dot(buf[2][HEADS_PER_SLAB * s], jnp.where(lo, v2, zero), preferred_element_type=F32)
                       + jnp.dot(buf[2][HEADS_PER_SLAB * s + 1], jnp.where(lo, zero, v2),
                                 preferred_element_type=F32))
        return pvs

    def stage4(n, buf, pvs):
        slabs, _ = item_coords(n)
        for s, slab in enumerate(slabs):
            o_scr[o_base + slab] = o_scr[o_base + slab] + pvs[s] * buf[3][s]

    def step(m, cur, nxt, diagonal):
        pvs = pv_dots(m, cur)
        incl = suffix_sums(nxt)
        stage3(m + 1, nxt, incl)
        qk_logits(m + 3, nxt)
        stage2(cur, diagonal)
        stage4(m, cur, pvs)

    qk_logits(0, buf_a)
    qk_logits(1, buf_b)
    stage2(buf_a, True)
    stage3(0, buf_a, suffix_sums(buf_a))
    qk_logits(2, buf_a)
    stage2(buf_b, True)
    for m in range(GROUPS - 1):
        cur, nxt = (buf_a, buf_b) if m % 2 == 0 else (buf_b, buf_a)
        step(m, cur, nxt, m + 2 < GROUPS)

    def step_group(t, _):
        m = GROUPS - 1 + GROUPS * t
        for r in range(0, GROUPS, 2):
            step(m + r, buf_b, buf_a, False)
            step(m + r + 1, buf_a, buf_b, False)
        return 0

    def step_group_pair(t, _):
        step_group(2 * t, 0)
        step_group(2 * t + 1, 0)
        return 0

    lax.fori_loop(0, i // 2, step_group_pair, 0)

    @pl.when(i % 2 == 1)
    def _():
        step_group(i - 1, 0)

    last = GROUPS * i + GROUPS - 1
    stage4(last, buf_b, pv_dots(last, buf_b))

    @pl.when(lax.rem(i, TILES_PER_EPILOGUE) == TILES_PER_EPILOGUE - 1)
    def _():
        o = jnp.concatenate(
            [jnp.concatenate([o_scr[t * N_SLABS + c] for c in range(N_SLABS)], axis=-1)
             for t in range(TILES_PER_EPILOGUE)], axis=0)
        z = z_ref[0]
        gated = o * (z * _sigmoid(z))
        x = x1_ref[0] + jnp.dot(gated.astype(BF16), w_out_ref[...], preferred_element_type=F32)
        pe = jnp.dot(p_ref[0, 0].astype(BF16), ple_w_ref[...], preferred_element_type=F32)
        gate = _sigmoid(jnp.dot(x.astype(BF16), gate_w_ref[...], preferred_element_type=F32))
        out_ref[0] = x + pe * gate


def _resident(shape):
    return pl.BlockSpec(shape, lambda b, t: (0,) * len(shape), pipeline_mode=pl.Buffered(1))


def kernel(x, p, a_norm, a_w_in, a_w_group, a_scale, a_w_out, kv_norm, w_kv, k_norm, b_norm, b_w_in, b_q_norm,
           b_w_out, ple_w, ple_gate_w):
    B, S, D = x.shape
    assert D == D_MODEL and S % ROW_TILE == 0 and S % Q_TILE == 0 and Q_TILE == K_TILE
    assert GROUPS % 2 == 0 and (S // Q_TILE) % TILES_PER_EPILOGUE == 0
    assert a_norm.shape[0] == 1 and b_norm.shape[0] == 1 and p.shape[0] == 2

    row = lambda a: a.reshape(1, D_MODEL).astype(F32)
    per_head = lambda a: jnp.tile(a.astype(F32), SB_HEADS).reshape(1, D_MODEL)
    wb = lambda a: a.astype(BF16)

    slab_shape = jax.ShapeDtypeStruct((B, N_SLABS, S, LANES), BF16)
    tok_spec = pl.BlockSpec((1, ROW_TILE, D_MODEL), lambda b, t: (b, t, 0))
    slab_spec = pl.BlockSpec((1, N_SLABS, ROW_TILE, LANES), lambda b, t: (b, 0, t, 0))
    vec_spec = _resident((1, D_MODEL))

    x1, q, k, v, z = pl.pallas_call(
        _layer0_kernel,
        grid=(B, S // ROW_TILE),
        in_specs=[
            tok_spec,
            pl.BlockSpec((1, 1, ROW_TILE, PLE_DIM), lambda b, t: (0, b, t, 0)),
            vec_spec, vec_spec, vec_spec, vec_spec, vec_spec, vec_spec,
            _resident((D_MODEL, 2 * D_MODEL)),
            _resident((len(POOL_WINDOWS), POOL_GROUP_DIM, POOL_GROUP_DIM)),
            _resident((D_MODEL, D_MODEL)),
            _resident((PLE_DIM, D_MODEL)),
            _resident((D_MODEL, D_MODEL)),
            _resident((D_MODEL, 2 * D_MODEL)),
            _resident((D_MODEL, 2 * D_MODEL)),
        ],
        out_specs=[tok_spec, slab_spec, slab_spec, slab_spec, tok_spec],
        out_shape=[jax.ShapeDtypeStruct((B, S, D_MODEL), F32), slab_shape, slab_shape, slab_shape,
                   jax.ShapeDtypeStruct((B, S, D_MODEL), F32)],
        scratch_shapes=[pltpu.VMEM((POOL_HALO, D_MODEL), F32)],
        compiler_params=pltpu.CompilerParams(
            dimension_semantics=("arbitrary", "arbitrary"), vmem_limit_bytes=VMEM_LIMIT_BYTES),
        name="layer0_and_projections",
    )(x, p, row(a_norm[0]), row(a_scale[0]), row(kv_norm), row(b_norm[0]), per_head(k_norm), per_head(b_q_norm[0]),
      wb(a_w_in[0]), wb(a_w_group[0]), wb(a_w_out[0]), wb(ple_w[0]), wb(ple_gate_w[0]), wb(w_kv), wb(b_w_in[0]))

    tri = (lax.broadcasted_iota(jnp.int32, (K_TILE, K_TILE), 0)
           >= lax.broadcasted_iota(jnp.int32, (K_TILE, K_TILE), 1)).astype(BF16)
    qtok_spec = pl.BlockSpec((1, TILES_PER_EPILOGUE * Q_TILE, D_MODEL), lambda b, i: (b, i // TILES_PER_EPILOGUE, 0))
    seq_spec = pl.BlockSpec((1, N_SLABS, S, LANES), lambda b, i: (b, 0, 0, 0))
    stage_buffers = [pltpu.VMEM((HEADS_PER_ITEM, Q_TILE, K_TILE), F32),
                     pltpu.VMEM((HEADS_PER_ITEM, Q_TILE, K_TILE), BF16),
                     pltpu.VMEM((HEADS_PER_ITEM, Q_TILE, K_TILE), BF16),
                     pltpu.VMEM((SLABS_PER_ITEM, Q_TILE, LANES), F32)]

    return pl.pallas_call(
        _attn_kernel,
        grid=(B, S // Q_TILE),
        in_specs=[
            pl.BlockSpec((1, N_SLABS, Q_TILE, LANES), lambda b, i: (b, 0, i, 0)),
            seq_spec, seq_spec,
            qtok_spec, qtok_spec,
            pl.BlockSpec((1, 1, TILES_PER_EPILOGUE * Q_TILE, PLE_DIM), lambda b, i: (1, b, i // TILES_PER_EPILOGUE, 0)),
            _resident((K_TILE, K_TILE)),
            _resident((D_MODEL, D_MODEL)),
            _resident((PLE_DIM, D_MODEL)),
            _resident((D_MODEL, D_MODEL)),
        ],
        out_specs=qtok_spec,
        out_shape=jax.ShapeDtypeStruct((B, S, D_MODEL), F32),
        scratch_shapes=[pltpu.VMEM((TILES_PER_EPILOGUE * N_SLABS, Q_TILE, LANES), F32),
                        pltpu.VMEM((N_SLABS, Q_TILE, LANES), F32),
                        pltpu.VMEM((SB_HEADS, Q_TILE, LANES), BF16)]
                       + stage_buffers + stage_buffers,
        compiler_params=pltpu.CompilerParams(
            dimension_semantics=("arbitrary", "arbitrary"), vmem_limit_bytes=VMEM_LIMIT_BYTES),
        name="stick_breaking_layer",
    )(q, k, v, z, x1, p, tri, wb(b_w_out[0]), wb(ple_w[1]), wb(ple_gate_w[1]))
```

```python
import functools

import jax
import jax.numpy as jnp
from jax import lax
from jax.experimental import pallas as pl
from jax.experimental.pallas import tpu as pltpu

D_MODEL = 1024
PLE_DIM = 256
POOL_WINDOWS = (2, 4, 8, 16)
POOL_GROUP_DIM = D_MODEL // len(POOL_WINDOWS)
POOL_HALO = 16
SB_HEADS = 16
SB_HEAD_DIM = D_MODEL // SB_HEADS
SB_SCALE = SB_HEAD_DIM ** -0.5
EPS = 1e-6
LOG2_E = 1.4426950408889634

LANES = 128
HEADS_PER_SLAB = LANES // SB_HEAD_DIM
N_SLABS = D_MODEL // LANES

ROW_TILE = 512
Q_TILE = 256
K_TILE = 256
SLABS_PER_ITEM = 1
HEADS_PER_ITEM = SLABS_PER_ITEM * HEADS_PER_SLAB
GROUPS = N_SLABS // SLABS_PER_ITEM
TILES_PER_EPILOGUE = 2
MASKED_LOG2 = -1e30
VMEM_LIMIT_BYTES = 56 * 1024 * 1024

F32 = jnp.float32
BF16 = jnp.bfloat16


def _sigmoid(t):
    return 1.0 / (1.0 + jnp.exp(-t))


def _rms_scale(t):
    return lax.rsqrt(jnp.mean(t * t, axis=-1, keepdims=True) + EPS)


def _head_rms(t):
    rows = t.shape[0]
    lo = lax.broadcasted_iota(jnp.int32, (rows, LANES), 1) < SB_HEAD_DIM
    outs = []
    for c in range(N_SLABS):
        blk = t[:, c * LANES:(c + 1) * LANES]
        sq = blk * blk
        s_lo = jnp.sum(jnp.where(lo, sq, 0.0), axis=-1, keepdims=True)
        s_hi = jnp.sum(jnp.where(lo, 0.0, sq), axis=-1, keepdims=True)
        ms = jnp.where(lo, s_lo, s_hi) * (1.0 / SB_HEAD_DIM)
        outs.append(blk * lax.rsqrt(ms + EPS))
    return jnp.concatenate(outs, axis=-1)


def _layer0_kernel(n_t, x_ref, p_ref, a_norm_ref, a_scale_ref, kv_norm_ref, b_norm_ref, k_gain_ref, q_gain_ref,
                   w_in_ref, w_group_ref, w_out_ref, ple_w_ref, gate_w_ref, w_kv_ref, bw_in_ref,
                   x1_ref, q_ref, k_ref, v_ref, z_ref, halo_ref, x1_scr):
    g = pl.program_id(0)
    t = lax.rem(jnp.minimum(g, pl.num_programs(0) - 2), n_t)
    rows = x_ref.shape[1]

    @pl.when(t == 0)
    def _():
        halo_ref[...] = jnp.zeros_like(halo_ref)

    @pl.when(g == 0)
    def _():
        x1_scr[...] = jnp.zeros_like(x1_scr)

    pe = jnp.dot(p_ref[0, 0].astype(BF16), ple_w_ref[...], preferred_element_type=F32)

    xp = x1_scr[...]
    x1_ref[0] = xp
    xn = xp * _rms_scale(xp)
    kv = jnp.dot((xn * kv_norm_ref[...]).astype(BF16), w_kv_ref[...], preferred_element_type=F32)

    x = x_ref[0]
    h = x * _rms_scale(x) * a_norm_ref[...]
    uz = jnp.dot(h.astype(BF16), w_in_ref[...], preferred_element_type=F32)
    u = uz[:, :D_MODEL]
    z = uz[:, D_MODEL:]

    xb = (xn * b_norm_ref[...]).astype(BF16)
    q_raw = jnp.dot(xb, bw_in_ref[:, :D_MODEL], preferred_element_type=F32)

    ext = jnp.concatenate([halo_ref[...], u], axis=0)
    halo_ref[...] = u[rows - POOL_HALO:, :]
    pos1 = t * rows + lax.broadcasted_iota(jnp.int32, (rows, 1), 0) + 1
    mixed = []
    for grp, w in enumerate(POOL_WINDOWS):
        e = ext[:, grp * POOL_GROUP_DIM:(grp + 1) * POOL_GROUP_DIM]
        s = e
        shift = 1
        while shift < w:
            s = s + pltpu.roll(s, shift, axis=0)
            shift *= 2
        inv_cnt = 1.0 / jnp.minimum(pos1, w).astype(F32)
        pooled = s[POOL_HALO:, :] * inv_cnt - e[POOL_HALO:, :]
        mixed.append(jnp.dot(pooled.astype(BF16), w_group_ref[grp], preferred_element_type=F32))
    mixed = jnp.concatenate(mixed, axis=-1) * a_scale_ref[...]
    gated = mixed * (z * _sigmoid(z))
    x = x + jnp.dot(gated.astype(BF16), w_out_ref[...], preferred_element_type=F32)

    k = (_head_rms(kv[:, :D_MODEL]) * k_gain_ref[...]).astype(BF16)
    v = kv[:, D_MODEL:].astype(BF16)
    q = (_head_rms(q_raw) * q_gain_ref[...] * (SB_SCALE * LOG2_E)).astype(BF16)
    for c in range(N_SLABS):
        sl = slice(c * LANES, (c + 1) * LANES)
        q_ref[0, c] = q[:, sl]
        k_ref[0, c] = k[:, sl]
        v_ref[0, c] = v[:, sl]

    gate = _sigmoid(jnp.dot(x.astype(BF16), gate_w_ref[...], preferred_element_type=F32))
    x1_scr[...] = x + pe * gate

    z_ref[0] = jnp.dot(xb, bw_in_ref[:, D_MODEL:], preferred_element_type=F32)


def _attn_kernel(q_ref, k_ref, v_ref, z_ref, x1_ref, p_ref, tri_ref, w_out_ref, ple_w_ref, gate_w_ref,
                 out_ref, o_scr, c_scr, qm_scr, lg_a, sp_a, a_a, dec_a, lg_b, sp_b, a_b, dec_b):
    i = pl.program_id(1)
    tq = q_ref.shape[2]
    lo = lax.broadcasted_iota(jnp.int32, (1, LANES), 1) < SB_HEAD_DIM
    row = lax.broadcasted_iota(jnp.int32, (tq, K_TILE), 0)
    col = lax.broadcasted_iota(jnp.int32, (tq, K_TILE), 1)
    causal = col < row
    tri = tri_ref[...]
    buf_a = (lg_a, sp_a, a_a, dec_a)
    buf_b = (lg_b, sp_b, a_b, dec_b)

    o_base = lax.rem(i, TILES_PER_EPILOGUE) * N_SLABS
    o_scr[pl.ds(o_base, N_SLABS)] = jnp.zeros((N_SLABS,) + o_scr.shape[1:], F32)
    c_scr[...] = jnp.zeros_like(c_scr)
    for slab in range(N_SLABS):
        q2 = q_ref[0, slab]
        zero = jnp.zeros_like(q2)
        qm_scr[HEADS_PER_SLAB * slab] = jnp.where(lo, q2, zero)
        qm_scr[HEADS_PER_SLAB * slab + 1] = jnp.where(lo, zero, q2)

    def item_coords(n):
        if isinstance(n, int):
            group, back = n % GROUPS, n // GROUPS
        else:
            group, back = lax.rem(n, GROUPS), lax.div(n, GROUPS)
        start = pl.multiple_of(jnp.maximum(i - back, 0) * K_TILE, K_TILE)
        return [SLABS_PER_ITEM * group + s for s in range(SLABS_PER_ITEM)], start

    def stage1(n, buf, diagonal):
        lg_buf, sp_buf = buf[0], buf[1]
        slabs, start = item_coords(n)
        for s, slab in enumerate(slabs):
            k2 = k_ref[0, slab, pl.ds(start, K_TILE), :]
            for hh in range(HEADS_PER_SLAB):
                h = HEADS_PER_SLAB * s + hh
                l = lax.dot_general(qm_scr[HEADS_PER_SLAB * slab + hh], k2, (((1,), (1,)), ((), ())),
                                    preferred_element_type=F32)
                sp = jnp.maximum(l, 0.0) + jnp.log2(1.0 + jnp.exp2(-jnp.abs(l)))
                if diagonal:
                    sp = jnp.where(causal, sp, 0.0)
                    l = jnp.where(causal, l, MASKED_LOG2)
                lg_buf[h] = l
                sp_buf[h] = sp.astype(BF16)

    def suffix_sums(buf):
        return [jnp.dot(buf[1][h], tri, preferred_element_type=F32) for h in range(HEADS_PER_ITEM)]

    def stage2(n, buf, incl):
        lg_buf, _, a_buf, dec_buf = buf
        for h in range(HEADS_PER_ITEM):
            a_buf[h] = jnp.exp2(lg_buf[h] - incl[h]).astype(BF16)
        slabs, _ = item_coords(n)
        for s, slab in enumerate(slabs):
            c = c_scr[slab]
            dec_buf[s] = jnp.exp2(-c)
            c_scr[slab] = c + jnp.where(lo, incl[HEADS_PER_SLAB * s][:, 0:1], incl[HEADS_PER_SLAB * s + 1][:, 0:1])

    def pv_dots(n, buf):
        slabs, start = item_coords(n)
        pvs = []
        for s, slab in enumerate(slabs):
            v2 = v_ref[0, slab, pl.ds(start, K_TILE), :]
            zero = jnp.zeros_like(v2)
            pvs.append(jnp.dot(buf[2][HEADS_PER_SLAB * s], jnp.where(lo, v2, zero), preferred_element_type=F32)
                       + jnp.dot(buf[2][HEADS_PER_SLAB * s + 1], jnp.where(lo, zero, v2),
                                 preferred_element_type=F32))
        return pvs

    def stage3(n, buf, pvs):
        slabs, _ = item_coords(n)
        for s, slab in enumerate(slabs):
            o_scr[o_base + slab] = o_scr[o_base + slab] + pvs[s] * buf[3][s]

    def step(m, cur, nxt, diagonal):
        pvs = pv_dots(m, cur)
        incl = suffix_sums(nxt)
        stage2(m + 1, nxt, incl)
        stage1(m + 2, cur, diagonal)
        stage3(m, cur, pvs)

    stage1(0, buf_a, True)
    stage2(0, buf_a, suffix_sums(buf_a))
    stage1(1, buf_b, True)
    for m in range(GROUPS - 1):
        cur, nxt = (buf_a, buf_b) if m % 2 == 0 else (buf_b, buf_a)
        step(m, cur, nxt, m + 2 < GROUPS)

    def step_group(t, _):
        m = GROUPS - 1 + GROUPS * t
        for r in range(0, GROUPS, 2):
            step(m + r, buf_b, buf_a, False)
            step(m + r + 1, buf_a, buf_b, False)
        return 0

    def step_group_pair(t, _):
        step_group(2 * t, 0)
        step_group(2 * t + 1, 0)
        return 0

    lax.fori_loop(0, i // 2, step_group_pair, 0)

    @pl.when(i % 2 == 1)
    def _():
        step_group(i - 1, 0)

    last = GROUPS * i + GROUPS - 1
    stage3(last, buf_b, pv_dots(last, buf_b))

    @pl.when(lax.rem(i, TILES_PER_EPILOGUE) == TILES_PER_EPILOGUE - 1)
    def _():
        o = jnp.concatenate(
            [jnp.concatenate([o_scr[t * N_SLABS + c] for c in range(N_SLABS)], axis=-1)
             for t in range(TILES_PER_EPILOGUE)], axis=0)
        z = z_ref[0]
        gated = o * (z * _sigmoid(z))
        x = x1_ref[0] + jnp.dot(gated.astype(BF16), w_out_ref[...], preferred_element_type=F32)
        pe = jnp.dot(p_ref[0, 0].astype(BF16), ple_w_ref[...], preferred_element_type=F32)
        gate = _sigmoid(jnp.dot(x.astype(BF16), gate_w_ref[...], preferred_element_type=F32))
        out_ref[0] = x + pe * gate


def _resident(shape):
    return pl.BlockSpec(shape, lambda *_: (0,) * len(shape), pipeline_mode=pl.Buffered(1))


def kernel(x, p, a_norm, a_w_in, a_w_group, a_scale, a_w_out, kv_norm, w_kv, k_norm, b_norm, b_w_in, b_q_norm,
           b_w_out, ple_w, ple_gate_w):
    B, S, D = x.shape
    assert D == D_MODEL and S % ROW_TILE == 0 and S % Q_TILE == 0 and Q_TILE == K_TILE
    assert GROUPS % 2 == 0 and (S // Q_TILE) % TILES_PER_EPILOGUE == 0
    assert a_norm.shape[0] == 1 and b_norm.shape[0] == 1 and p.shape[0] == 2

    row = lambda a: a.reshape(1, D_MODEL).astype(F32)
    per_head = lambda a: jnp.tile(a.astype(F32), SB_HEADS).reshape(1, D_MODEL)
    wb = lambda a: a.astype(BF16)

    slab_shape = jax.ShapeDtypeStruct((B, N_SLABS, S, LANES), BF16)
    n_t = S // ROW_TILE
    n_tiles = B * n_t
    cur = lambda g: jnp.minimum(g, n_tiles - 1)
    prev = lambda g: jnp.maximum(g - 1, 0)
    tok_spec = pl.BlockSpec((1, ROW_TILE, D_MODEL), lambda g: (prev(g) // n_t, prev(g) % n_t, 0))
    slab_spec = pl.BlockSpec((1, N_SLABS, ROW_TILE, LANES), lambda g: (prev(g) // n_t, 0, prev(g) % n_t, 0))
    vec_spec = _resident((1, D_MODEL))

    x1, q, k, v, z = pl.pallas_call(
        functools.partial(_layer0_kernel, n_t),
        grid=(n_tiles + 1,),
        in_specs=[
            pl.BlockSpec((1, ROW_TILE, D_MODEL), lambda g: (cur(g) // n_t, cur(g) % n_t, 0)),
            pl.BlockSpec((1, 1, ROW_TILE, PLE_DIM), lambda g: (0, cur(g) // n_t, cur(g) % n_t, 0)),
            vec_spec, vec_spec, vec_spec, vec_spec, vec_spec, vec_spec,
            _resident((D_MODEL, 2 * D_MODEL)),
            _resident((len(POOL_WINDOWS), POOL_GROUP_DIM, POOL_GROUP_DIM)),
            _resident((D_MODEL, D_MODEL)),
            _resident((PLE_DIM, D_MODEL)),
            _resident((D_MODEL, D_MODEL)),
            _resident((D_MODEL, 2 * D_MODEL)),
            _resident((D_MODEL, 2 * D_MODEL)),
        ],
        out_specs=[tok_spec, slab_spec, slab_spec, slab_spec, tok_spec],
        out_shape=[jax.ShapeDtypeStruct((B, S, D_MODEL), F32), slab_shape, slab_shape, slab_shape,
                   jax.ShapeDtypeStruct((B, S, D_MODEL), F32)],
        scratch_shapes=[pltpu.VMEM((POOL_HALO, D_MODEL), F32),
                        pltpu.VMEM((ROW_TILE, D_MODEL), F32)],
        compiler_params=pltpu.CompilerParams(
            dimension_semantics=("arbitrary",), vmem_limit_bytes=VMEM_LIMIT_BYTES),
        name="layer0_and_projections",
    )(x, p, row(a_norm[0]), row(a_scale[0]), row(kv_norm), row(b_norm[0]), per_head(k_norm), per_head(b_q_norm[0]),
      wb(a_w_in[0]), wb(a_w_group[0]), wb(a_w_out[0]), wb(ple_w[0]), wb(ple_gate_w[0]), wb(w_kv), wb(b_w_in[0]))

    tri = (lax.broadcasted_iota(jnp.int32, (K_TILE, K_TILE), 0)
           >= lax.broadcasted_iota(jnp.int32, (K_TILE, K_TILE), 1)).astype(BF16)
    qtok_spec = pl.BlockSpec((1, TILES_PER_EPILOGUE * Q_TILE, D_MODEL), lambda b, i: (b, i // TILES_PER_EPILOGUE, 0))
    seq_spec = pl.BlockSpec((1, N_SLABS, S, LANES), lambda b, i: (b, 0, 0, 0))
    stage_buffers = [pltpu.VMEM((HEADS_PER_ITEM, Q_TILE, K_TILE), F32),
                     pltpu.VMEM((HEADS_PER_ITEM, Q_TILE, K_TILE), BF16),
                     pltpu.VMEM((HEADS_PER_ITEM, Q_TILE, K_TILE), BF16),
                     pltpu.VMEM((SLABS_PER_ITEM, Q_TILE, LANES), F32)]

    return pl.pallas_call(
        _attn_kernel,
        grid=(B, S // Q_TILE),
        in_specs=[
            pl.BlockSpec((1, N_SLABS, Q_TILE, LANES), lambda b, i: (b, 0, i, 0)),
            seq_spec, seq_spec,
            qtok_spec, qtok_spec,
            pl.BlockSpec((1, 1, TILES_PER_EPILOGUE * Q_TILE, PLE_DIM), lambda b, i: (1, b, i // TILES_PER_EPILOGUE, 0)),
            _resident((K_TILE, K_TILE)),
            _resident((D_MODEL, D_MODEL)),
            _resident((PLE_DIM, D_MODEL)),
            _resident((D_MODEL, D_MODEL)),
        ],
        out_specs=qtok_spec,
        out_shape=jax.ShapeDtypeStruct((B, S, D_MODEL), F32),
        scratch_shapes=[pltpu.VMEM((TILES_PER_EPILOGUE * N_SLABS, Q_TILE, LANES), F32),
                        pltpu.VMEM((N_SLABS, Q_TILE, LANES), F32),
                        pltpu.VMEM((SB_HEADS, Q_TILE, LANES), BF16)]
                       + stage_buffers + stage_buffers,
        compiler_params=pltpu.CompilerParams(
            dimension_semantics=("arbitrary", "arbitrary"), vmem_limit_bytes=VMEM_LIMIT_BYTES),
        name="stick_breaking_layer",
    )(q, k, v, z, x1, p, tri, wb(b_w_out[0]), wb(ple_w[1]), wb(ple_gate_w[1]))
```

```python
import jax
import jax.numpy as jnp
from jax import lax
from jax.experimental import pallas as pl
from jax.experimental.pallas import tpu as pltpu

D_MODEL = 1024
PLE_DIM = 256
POOL_WINDOWS = (2, 4, 8, 16)
POOL_GROUP_DIM = D_MODEL // len(POOL_WINDOWS)
POOL_HALO = 16
SB_HEADS = 16
SB_HEAD_DIM = D_MODEL // SB_HEADS
SB_SCALE = SB_HEAD_DIM ** -0.5
EPS = 1e-6
LOG2_E = 1.4426950408889634

LANES = 128
HEADS_PER_SLAB = LANES // SB_HEAD_DIM
N_SLABS = D_MODEL // LANES

ROW_TILE = 512
Q_TILE = 256
K_TILE = 256
SLABS_PER_ITEM = 1
HEADS_PER_ITEM = SLABS_PER_ITEM * HEADS_PER_SLAB
GROUPS = N_SLABS // SLABS_PER_ITEM
TILES_PER_EPILOGUE = 2
MASKED_LOG2 = -1e30
VEC_ROWS = 8
VMEM_LIMIT_BYTES = 56 * 1024 * 1024

F32 = jnp.float32
BF16 = jnp.bfloat16


def _sigmoid(t):
    return 1.0 / (1.0 + jnp.exp(-t))


def _rms_scale(t):
    return lax.rsqrt(jnp.mean(t * t, axis=-1, keepdims=True) + EPS)


def _head_rms(t):
    rows = t.shape[0]
    lo = lax.broadcasted_iota(jnp.int32, (rows, LANES), 1) < SB_HEAD_DIM
    outs = []
    for c in range(N_SLABS):
        blk = t[:, c * LANES:(c + 1) * LANES]
        sq = blk * blk
        s_lo = jnp.sum(jnp.where(lo, sq, 0.0), axis=-1, keepdims=True)
        s_hi = jnp.sum(jnp.where(lo, 0.0, sq), axis=-1, keepdims=True)
        ms = jnp.where(lo, s_lo, s_hi) * (1.0 / SB_HEAD_DIM)
        outs.append(blk * lax.rsqrt(ms + EPS))
    return jnp.concatenate(outs, axis=-1)


def _layer0_kernel(x_ref, p_ref, vec_ref,
                   w_in_ref, w_group_ref, w_out_ref, ple_w_ref, gate_w_ref, w_kv_ref, bw_in_ref,
                   x1_ref, q_ref, k_ref, v_ref, z_ref, halo_ref):
    t = pl.program_id(1)
    rows = x_ref.shape[1]
    a_norm, a_scale, kv_norm, b_norm, k_gain, q_gain = (vec_ref[r:r + 1, :] for r in range(6))

    @pl.when(t == 0)
    def _():
        halo_ref[...] = jnp.zeros_like(halo_ref)

    x = x_ref[0]
    h = x * _rms_scale(x) * a_norm
    uz = jnp.dot(h.astype(BF16), w_in_ref[...], preferred_element_type=F32)
    u = uz[:, :D_MODEL]
    z = uz[:, D_MODEL:]

    ext = jnp.concatenate([halo_ref[...], u], axis=0)
    halo_ref[...] = u[rows - POOL_HALO:, :]
    pos1 = t * rows + lax.broadcasted_iota(jnp.int32, (rows, 1), 0) + 1
    mixed = []
    for g, w in enumerate(POOL_WINDOWS):
        e = ext[:, g * POOL_GROUP_DIM:(g + 1) * POOL_GROUP_DIM]
        s = e
        shift = 1
        while shift < w:
            s = s + pltpu.roll(s, shift, axis=0)
            shift *= 2
        inv_cnt = 1.0 / jnp.minimum(pos1, w).astype(F32)
        pooled = s[POOL_HALO:, :] * inv_cnt - e[POOL_HALO:, :]
        mixed.append(jnp.dot(pooled.astype(BF16), w_group_ref[g], preferred_element_type=F32))
    mixed = jnp.concatenate(mixed, axis=-1) * a_scale
    gated = mixed * (z * _sigmoid(z))
    x = x + jnp.dot(gated.astype(BF16), w_out_ref[...], preferred_element_type=F32)

    pe = jnp.dot(p_ref[0, 0].astype(BF16), ple_w_ref[...], preferred_element_type=F32)
    gate = _sigmoid(jnp.dot(x.astype(BF16), gate_w_ref[...], preferred_element_type=F32))
    x = x + pe * gate
    x1_ref[0] = x

    xn = x * _rms_scale(x)
    kv = jnp.dot((xn * kv_norm).astype(BF16), w_kv_ref[...], preferred_element_type=F32)
    qz = jnp.dot((xn * b_norm).astype(BF16), bw_in_ref[...], preferred_element_type=F32)
    k = (_head_rms(kv[:, :D_MODEL]) * k_gain).astype(BF16)
    v = kv[:, D_MODEL:].astype(BF16)
    q = (_head_rms(qz[:, :D_MODEL]) * q_gain * (SB_SCALE * LOG2_E)).astype(BF16)
    z_ref[0] = qz[:, D_MODEL:]
    for c in range(N_SLABS):
        sl = slice(c * LANES, (c + 1) * LANES)
        q_ref[0, c] = q[:, sl]
        k_ref[0, c] = k[:, sl]
        v_ref[0, c] = v[:, sl]


def _attn_kernel(q_ref, k_ref, v_ref, z_ref, x1_ref, p_ref, tri_ref, w_out_ref, ple_w_ref, gate_w_ref,
                 out_ref, o_scr, c_scr, qm_scr, lg_a, sp_a, a_a, dec_a, lg_b, sp_b, a_b, dec_b):
    i = pl.program_id(1)
    tq = q_ref.shape[2]
    lo = lax.broadcasted_iota(jnp.int32, (1, LANES), 1) < SB_HEAD_DIM
    row = lax.broadcasted_iota(jnp.int32, (tq, K_TILE), 0)
    col = lax.broadcasted_iota(jnp.int32, (tq, K_TILE), 1)
    causal = col < row
    tri = tri_ref[...]
    buf_a = (lg_a, sp_a, a_a, dec_a)
    buf_b = (lg_b, sp_b, a_b, dec_b)

    o_base = lax.rem(i, TILES_PER_EPILOGUE) * N_SLABS
    for slab in range(N_SLABS - SLABS_PER_ITEM, N_SLABS):
        o_scr[o_base + slab] = jnp.zeros(o_scr.shape[1:], F32)
    for slab in range(N_SLABS):
        q2 = q_ref[0, slab]
        zero = jnp.zeros_like(q2)
        qm_scr[HEADS_PER_SLAB * slab] = jnp.where(lo, q2, zero)
        qm_scr[HEADS_PER_SLAB * slab + 1] = jnp.where(lo, zero, q2)

    def item_coords(n):
        if isinstance(n, int):
            group, back = n % GROUPS, n // GROUPS
        else:
            group, back = lax.rem(n, GROUPS), lax.div(n, GROUPS)
        start = pl.multiple_of(jnp.maximum(i - back, 0) * K_TILE, K_TILE)
        return [SLABS_PER_ITEM * group + s for s in range(SLABS_PER_ITEM)], start

    def qk_logits(n, buf):
        slabs, start = item_coords(n)
        for s, slab in enumerate(slabs):
            k2 = k_ref[0, slab, pl.ds(start, K_TILE), :]
            for hh in range(HEADS_PER_SLAB):
                buf[0][HEADS_PER_SLAB * s + hh] = lax.dot_general(
                    qm_scr[HEADS_PER_SLAB * slab + hh], k2, (((1,), (1,)), ((), ())), preferred_element_type=F32)

    def stage2(buf, diagonal):
        lg_buf, sp_buf = buf[0], buf[1]
        for h in range(HEADS_PER_ITEM):
            l = lg_buf[h]
            sp = jnp.maximum(l, 0.0) + jnp.log2(1.0 + jnp.exp2(-jnp.abs(l)))
            if diagonal:
                sp = jnp.where(causal, sp, 0.0)
                lg_buf[h] = jnp.where(causal, l, MASKED_LOG2)
            sp_buf[h] = sp.astype(BF16)

    def suffix_sums(buf):
        return [jnp.dot(buf[1][h], tri, preferred_element_type=F32) for h in range(HEADS_PER_ITEM)]

    def stage3(n, buf, incl, first):
        lg_buf, _, a_buf, dec_buf = buf
        for h in range(HEADS_PER_ITEM):
            a_buf[h] = jnp.exp2(lg_buf[h] - incl[h]).astype(BF16)
        slabs, _ = item_coords(n)
        for s, slab in enumerate(slabs):
            row_sum = jnp.where(lo, incl[HEADS_PER_SLAB * s][:, 0:1], incl[HEADS_PER_SLAB * s + 1][:, 0:1])
            if first:
                dec_buf[s] = jnp.ones(dec_buf.shape[1:], F32)
                c_scr[slab] = row_sum
            else:
                c = c_scr[slab]
                dec_buf[s] = jnp.exp2(-c)
                c_scr[slab] = c + row_sum

    def pv_dots(n, buf):
        slabs, start = item_coords(n)
        pvs = []
        for s, slab in enumerate(slabs):
            v2 = v_ref[0, slab, pl.ds(start, K_TILE), :]
            zero = jnp.zeros_like(v2)
            pvs.append(jnp.dot(buf[2][HEADS_PER_SLAB * s], jnp.where(lo, v2, zero), preferred_element_type=F32)
                       + jnp.dot(buf[2][HEADS_PER_SLAB * s + 1], jnp.where(lo, zero, v2),
                                 preferred_element_type=F32))
        return pvs

    def stage4(n, buf, pvs, first):
        slabs, _ = item_coords(n)
        for s, slab in enumerate(slabs):
            if first:
                o_scr[o_base + slab] = pvs[s]
            else:
                o_scr[o_base + slab] = o_scr[o_base + slab] + pvs[s] * buf[3][s]

    def step(m, cur, nxt, static):
        pvs = pv_dots(m, cur)
        incl = suffix_sums(nxt)
        stage3(m + 1, nxt, incl, static and m + 1 < GROUPS)
        qk_logits(m + 3, nxt)
        stage2(cur, static and m + 2 < GROUPS)
        stage4(m, cur, pvs, static and m < GROUPS - 1)

    qk_logits(0, buf_a)
    qk_logits(1, buf_b)
    stage2(buf_a, True)
    stage3(0, buf_a, suffix_sums(buf_a), True)
    qk_logits(2, buf_a)
    stage2(buf_b, True)
    for m in range(GROUPS - 1):
        cur, nxt = (buf_a, buf_b) if m % 2 == 0 else (buf_b, buf_a)
        step(m, cur, nxt, True)

    def step_group(t, _):
        m = GROUPS - 1 + GROUPS * t
        for r in range(0, GROUPS, 2):
            step(m + r, buf_b, buf_a, False)
            step(m + r + 1, buf_a, buf_b, False)
        return 0

    def step_group_pair(t, _):
        step_group(2 * t, 0)
        step_group(2 * t + 1, 0)
        return 0

    lax.fori_loop(0, i // 2, step_group_pair, 0)

    @pl.when(i % 2 == 1)
    def _():
        step_group(i - 1, 0)

    last = GROUPS * i + GROUPS - 1
    stage4(last, buf_b, pv_dots(last, buf_b), False)

    @pl.when(lax.rem(i, TILES_PER_EPILOGUE) == TILES_PER_EPILOGUE - 1)
    def _():
        o = jnp.concatenate(
            [jnp.concatenate([o_scr[t * N_SLABS + c] for c in range(N_SLABS)], axis=-1)
             for t in range(TILES_PER_EPILOGUE)], axis=0)
        z = z_ref[0]
        gated = o * (z * _sigmoid(z))
        x = x1_ref[0] + jnp.dot(gated.astype(BF16), w_out_ref[...], preferred_element_type=F32)
        pe = jnp.dot(p_ref[0, 0].astype(BF16), ple_w_ref[...], preferred_element_type=F32)
        gate = _sigmoid(jnp.dot(x.astype(BF16), gate_w_ref[...], preferred_element_type=F32))
        out_ref[0] = x + pe * gate


def _resident(shape):
    return pl.BlockSpec(shape, lambda b, t: (0,) * len(shape), pipeline_mode=pl.Buffered(1))


def kernel(x, p, a_norm, a_w_in, a_w_group, a_scale, a_w_out, kv_norm, w_kv, k_norm, b_norm, b_w_in, b_q_norm,
           b_w_out, ple_w, ple_gate_w):
    B, S, D = x.shape
    assert D == D_MODEL and S % ROW_TILE == 0 and S % Q_TILE == 0 and Q_TILE == K_TILE
    assert GROUPS % 2 == 0 and (S // Q_TILE) % TILES_PER_EPILOGUE == 0
    assert a_norm.shape[0] == 1 and b_norm.shape[0] == 1 and p.shape[0] == 2

    per_head = lambda a: jnp.tile(a, SB_HEADS)
    vec_rows = [a_norm[0], a_scale[0], kv_norm, b_norm[0], per_head(k_norm), per_head(b_q_norm[0])]
    vecs = jnp.stack([r.astype(F32) for r in vec_rows] + [jnp.zeros((D_MODEL,), F32)] * (VEC_ROWS - len(vec_rows)))
    wb = lambda a: a.astype(BF16)

    slab_shape = jax.ShapeDtypeStruct((B, N_SLABS, S, LANES), BF16)
    tok_spec = pl.BlockSpec((1, ROW_TILE, D_MODEL), lambda b, t: (b, t, 0))
    slab_spec = pl.BlockSpec((1, N_SLABS, ROW_TILE, LANES), lambda b, t: (b, 0, t, 0))

    x1, q, k, v, z = pl.pallas_call(
        _layer0_kernel,
        grid=(B, S // ROW_TILE),
        in_specs=[
            tok_spec,
            pl.BlockSpec((1, 1, ROW_TILE, PLE_DIM), lambda b, t: (0, b, t, 0)),
            _resident((VEC_ROWS, D_MODEL)),
            _resident((D_MODEL, 2 * D_MODEL)),
            _resident((len(POOL_WINDOWS), POOL_GROUP_DIM, POOL_GROUP_DIM)),
            _resident((D_MODEL, D_MODEL)),
            _resident((PLE_DIM, D_MODEL)),
            _resident((D_MODEL, D_MODEL)),
            _resident((D_MODEL, 2 * D_MODEL)),
            _resident((D_MODEL, 2 * D_MODEL)),
        ],
        out_specs=[tok_spec, slab_spec, slab_spec, slab_spec, tok_spec],
        out_shape=[jax.ShapeDtypeStruct((B, S, D_MODEL), F32), slab_shape, slab_shape, slab_shape,
                   jax.ShapeDtypeStruct((B, S, D_MODEL), F32)],
        scratch_shapes=[pltpu.VMEM((POOL_HALO, D_MODEL), F32)],
        compiler_params=pltpu.CompilerParams(
            dimension_semantics=("arbitrary", "arbitrary"), vmem_limit_bytes=VMEM_LIMIT_BYTES),
        name="layer0_and_projections",
    )(x, p, vecs,
      wb(a_w_in[0]), wb(a_w_group[0]), wb(a_w_out[0]), wb(ple_w[0]), wb(ple_gate_w[0]), wb(w_kv), wb(b_w_in[0]))

    tri = (lax.broadcasted_iota(jnp.int32, (K_TILE, K_TILE), 0)
           >= lax.broadcasted_iota(jnp.int32, (K_TILE, K_TILE), 1)).astype(BF16)
    qtok_spec = pl.BlockSpec((1, TILES_PER_EPILOGUE * Q_TILE, D_MODEL), lambda b, i: (b, i // TILES_PER_EPILOGUE, 0))
    seq_spec = pl.BlockSpec((1, N_SLABS, S, LANES), lambda b, i: (b, 0, 0, 0))
    stage_buffers = [pltpu.VMEM((HEADS_PER_ITEM, Q_TILE, K_TILE), F32),
                     pltpu.VMEM((HEADS_PER_ITEM, Q_TILE, K_TILE), BF16),
                     pltpu.VMEM((HEADS_PER_ITEM, Q_TILE, K_TILE), BF16),
                     pltpu.VMEM((SLABS_PER_ITEM, Q_TILE, LANES), F32)]

    return pl.pallas_call(
        _attn_kernel,
        grid=(B, S // Q_TILE),
        in_specs=[
            pl.BlockSpec((1, N_SLABS, Q_TILE, LANES), lambda b, i: (b, 0, i, 0)),
            seq_spec, seq_spec,
            qtok_spec, qtok_spec,
            pl.BlockSpec((1, 1, TILES_PER_EPILOGUE * Q_TILE, PLE_DIM), lambda b, i: (1, b, i // TILES_PER_EPILOGUE, 0)),
            _resident((K_TILE, K_TILE)),
            _resident((D_MODEL, D_MODEL)),
            _resident((PLE_DIM, D_MODEL)),
            _resident((D_MODEL, D_MODEL)),
        ],
        out_specs=qtok_spec,
        out_shape=jax.ShapeDtypeStruct((B, S, D_MODEL), F32),
        scratch_shapes=[pltpu.VMEM((TILES_PER_EPILOGUE * N_SLABS, Q_TILE, LANES), F32),
                        pltpu.VMEM((N_SLABS, Q_TILE, LANES), F32),
                        pltpu.VMEM((SB_HEADS, Q_TILE, LANES), BF16)]
                       + stage_buffers + stage_buffers,
        compiler_params=pltpu.CompilerParams(
            dimension_semantics=("arbitrary", "arbitrary"), vmem_limit_bytes=VMEM_LIMIT_BYTES),
        name="stick_breaking_layer",
    )(q, k, v, z, x1, p, tri, wb(b_w_out[0]), wb(ple_w[1]), wb(ple_gate_w[1]))
```

```python
import jax
import jax.numpy as jnp
from jax import lax
from jax.experimental import pallas as pl
from jax.experimental.pallas import tpu as pltpu

D_MODEL = 1024
PLE_DIM = 256
POOL_WINDOWS = (2, 4, 8, 16)
POOL_GROUP_DIM = D_MODEL // len(POOL_WINDOWS)
POOL_HALO = 16
SB_HEADS = 16
SB_HEAD_DIM = D_MODEL // SB_HEADS
SB_SCALE = SB_HEAD_DIM ** -0.5
EPS = 1e-6
LOG2_E = 1.4426950408889634

LANES = 128
HEADS_PER_SLAB = LANES // SB_HEAD_DIM
N_SLABS = D_MODEL // LANES

ROW_TILE = 512
Q_TILE = 256
K_TILE = 256
SLABS_PER_ITEM = 1
HEADS_PER_ITEM = SLABS_PER_ITEM * HEADS_PER_SLAB
GROUPS = N_SLABS // SLABS_PER_ITEM
TILES_PER_EPILOGUE = 2
MASKED_LOG2 = -1e30
WEIGHT_CHUNK_ROWS = 256
VEC_ROWS = 8
VMEM_LIMIT_BYTES = 56 * 1024 * 1024

F32 = jnp.float32
BF16 = jnp.bfloat16


def _sigmoid(t):
    return 1.0 / (1.0 + jnp.exp(-t))


def _rms_scale(t):
    return lax.rsqrt(jnp.mean(t * t, axis=-1, keepdims=True) + EPS)


def _head_rms(t):
    rows = t.shape[0]
    lo = lax.broadcasted_iota(jnp.int32, (rows, LANES), 1) < SB_HEAD_DIM
    outs = []
    for c in range(N_SLABS):
        blk = t[:, c * LANES:(c + 1) * LANES]
        sq = blk * blk
        s_lo = jnp.sum(jnp.where(lo, sq, 0.0), axis=-1, keepdims=True)
        s_hi = jnp.sum(jnp.where(lo, 0.0, sq), axis=-1, keepdims=True)
        ms = jnp.where(lo, s_lo, s_hi) * (1.0 / SB_HEAD_DIM)
        outs.append(blk * lax.rsqrt(ms + EPS))
    return jnp.concatenate(outs, axis=-1)


def _weight_chunks(hbm_refs, vmem_refs):
    w_in, w_group, w_out, ple_w, gate_w, w_kv, bw_in = hbm_refs
    sources = [w_in.at[0], w_out.at[0], ple_w.at[0], gate_w.at[0], w_kv, bw_in.at[0]]
    sources += [w_group.at[0, g] for g in range(len(POOL_WINDOWS))]
    dests = [vmem_refs[0], vmem_refs[2], vmem_refs[3], vmem_refs[4], vmem_refs[5], vmem_refs[6]]
    dests += [vmem_refs[1].at[g] for g in range(len(POOL_WINDOWS))]
    chunks = []
    for src, dst in zip(sources, dests):
        n_rows, n_cols = dst.shape
        for r0 in range(0, n_rows, WEIGHT_CHUNK_ROWS):
            nr = min(WEIGHT_CHUNK_ROWS, n_rows - r0)
            chunks.append((src.at[pl.ds(r0, nr), :], dst.at[pl.ds(r0, nr), :], nr, n_cols))
    return chunks


def _load_weights(hbm_refs, vmem_refs, stage_ref, sems):
    chunks = _weight_chunks(hbm_refs, vmem_refs)

    def copy(c):
        src, _, nr, nc = chunks[c]
        slot = c % 2
        return pltpu.make_async_copy(src, stage_ref.at[slot, pl.ds(0, nr), pl.ds(0, nc)], sems.at[slot])

    copy(0).start()
    for c, (_, dst, nr, nc) in enumerate(chunks):
        if c + 1 < len(chunks):
            copy(c + 1).start()
        copy(c).wait()
        dst[...] = stage_ref[c % 2, :nr, :nc].astype(BF16)


def _layer0_kernel(x_ref, p_ref, vec_ref,
                   w_in_hbm, w_group_hbm, w_out_hbm, ple_w_hbm, gate_w_hbm, w_kv_hbm, bw_in_hbm,
                   x1_ref, q_ref, k_ref, v_ref, z_ref, halo_ref,
                   w_in_ref, w_group_ref, w_out_ref, ple_w_ref, gate_w_ref, w_kv_ref, bw_in_ref, stage_ref, sems):
    t = pl.program_id(1)
    rows = x_ref.shape[1]

    @pl.when((pl.program_id(0) == 0) & (t == 0))
    def _():
        _load_weights((w_in_hbm, w_group_hbm, w_out_hbm, ple_w_hbm, gate_w_hbm, w_kv_hbm, bw_in_hbm),
                      (w_in_ref, w_group_ref, w_out_ref, ple_w_ref, gate_w_ref, w_kv_ref, bw_in_ref),
                      stage_ref, sems)

    a_norm, a_scale, kv_norm, b_norm, k_gain, q_gain = (vec_ref[r:r + 1, :] for r in range(6))

    @pl.when(t == 0)
    def _():
        halo_ref[...] = jnp.zeros_like(halo_ref)

    x = x_ref[0]
    h = x * _rms_scale(x) * a_norm
    uz = jnp.dot(h.astype(BF16), w_in_ref[...], preferred_element_type=F32)
    u = uz[:, :D_MODEL]
    z = uz[:, D_MODEL:]

    ext = jnp.concatenate([halo_ref[...], u], axis=0)
    halo_ref[...] = u[rows - POOL_HALO:, :]
    pos1 = t * rows + lax.broadcasted_iota(jnp.int32, (rows, 1), 0) + 1
    mixed = []
    for g, w in enumerate(POOL_WINDOWS):
        e = ext[:, g * POOL_GROUP_DIM:(g + 1) * POOL_GROUP_DIM]
        s = e
        shift = 1
        while shift < w:
            s = s + pltpu.roll(s, shift, axis=0)
            shift *= 2
        inv_cnt = 1.0 / jnp.minimum(pos1, w).astype(F32)
        pooled = s[POOL_HALO:, :] * inv_cnt - e[POOL_HALO:, :]
        mixed.append(jnp.dot(pooled.astype(BF16), w_group_ref[g], preferred_element_type=F32))
    mixed = jnp.concatenate(mixed, axis=-1) * a_scale
    gated = mixed * (z * _sigmoid(z))
    x = x + jnp.dot(gated.astype(BF16), w_out_ref[...], preferred_element_type=F32)

    pe = jnp.dot(p_ref[0, 0].astype(BF16), ple_w_ref[...], preferred_element_type=F32)
    gate = _sigmoid(jnp.dot(x.astype(BF16), gate_w_ref[...], preferred_element_type=F32))
    x = x + pe * gate
    x1_ref[0] = x

    xn = x * _rms_scale(x)
    kv = jnp.dot((xn * kv_norm).astype(BF16), w_kv_ref[...], preferred_element_type=F32)
    qz = jnp.dot((xn * b_norm).astype(BF16), bw_in_ref[...], preferred_element_type=F32)
    k = (_head_rms(kv[:, :D_MODEL]) * k_gain).astype(BF16)
    v = kv[:, D_MODEL:].astype(BF16)
    q = (_head_rms(qz[:, :D_MODEL]) * q_gain * (SB_SCALE * LOG2_E)).astype(BF16)
    z_ref[0] = qz[:, D_MODEL:]
    for c in range(N_SLABS):
        sl = slice(c * LANES, (c + 1) * LANES)
        q_ref[0, c] = q[:, sl]
        k_ref[0, c] = k[:, sl]
        v_ref[0, c] = v[:, sl]


def _attn_kernel(q_ref, k_ref, v_ref, z_ref, x1_ref, p_ref, tri_ref, w_out_ref, ple_w_ref, gate_w_ref,
                 out_ref, o_scr, c_scr, qm_scr, lg_a, sp_a, a_a, dec_a, lg_b, sp_b, a_b, dec_b):
    i = pl.program_id(1)
    tq = q_ref.shape[2]
    lo = lax.broadcasted_iota(jnp.int32, (1, LANES), 1) < SB_HEAD_DIM
    row = lax.broadcasted_iota(jnp.int32, (tq, K_TILE), 0)
    col = lax.broadcasted_iota(jnp.int32, (tq, K_TILE), 1)
    causal = col < row
    tri = tri_ref[...]
    buf_a = (lg_a, sp_a, a_a, dec_a)
    buf_b = (lg_b, sp_b, a_b, dec_b)

    o_base = lax.rem(i, TILES_PER_EPILOGUE) * N_SLABS
    for slab in range(N_SLABS - SLABS_PER_ITEM, N_SLABS):
        o_scr[o_base + slab] = jnp.zeros(o_scr.shape[1:], F32)
    for slab in range(N_SLABS):
        q2 = q_ref[0, slab]
        zero = jnp.zeros_like(q2)
        qm_scr[HEADS_PER_SLAB * slab] = jnp.where(lo, q2, zero)
        qm_scr[HEADS_PER_SLAB * slab + 1] = jnp.where(lo, zero, q2)

    def item_coords(n):
        if isinstance(n, int):
            group, back = n % GROUPS, n // GROUPS
        else:
            group, back = lax.rem(n, GROUPS), lax.div(n, GROUPS)
        start = pl.multiple_of(jnp.maximum(i - back, 0) * K_TILE, K_TILE)
        return [SLABS_PER_ITEM * group + s for s in range(SLABS_PER_ITEM)], start

    def qk_logits(n, buf):
        slabs, start = item_coords(n)
        for s, slab in enumerate(slabs):
            k2 = k_ref[0, slab, pl.ds(start, K_TILE), :]
            for hh in range(HEADS_PER_SLAB):
                buf[0][HEADS_PER_SLAB * s + hh] = lax.dot_general(
                    qm_scr[HEADS_PER_SLAB * slab + hh], k2, (((1,), (1,)), ((), ())), preferred_element_type=F32)

    def stage2(buf, diagonal):
        lg_buf, sp_buf = buf[0], buf[1]
        for h in range(HEADS_PER_ITEM):
            l = lg_buf[h]
            sp = jnp.maximum(l, 0.0) + jnp.log2(1.0 + jnp.exp2(-jnp.abs(l)))
            if diagonal:
                sp = jnp.where(causal, sp, 0.0)
                lg_buf[h] = jnp.where(causal, l, MASKED_LOG2)
            sp_buf[h] = sp.astype(BF16)

    def suffix_sums(buf):
        return [jnp.dot(buf[1][h], tri, preferred_element_type=F32) for h in range(HEADS_PER_ITEM)]

    def stage3(n, buf, incl, first):
        lg_buf, _, a_buf, dec_buf = buf
        for h in range(HEADS_PER_ITEM):
            a_buf[h] = jnp.exp2(lg_buf[h] - incl[h]).astype(BF16)
        slabs, _ = item_coords(n)
        for s, slab in enumerate(slabs):
            row_sum = jnp.where(lo, incl[HEADS_PER_SLAB * s][:, 0:1], incl[HEADS_PER_SLAB * s + 1][:, 0:1])
            if first:
                dec_buf[s] = jnp.ones(dec_buf.shape[1:], F32)
                c_scr[slab] = row_sum
            else:
                c = c_scr[slab]
                dec_buf[s] = jnp.exp2(-c)
                c_scr[slab] = c + row_sum

    def pv_dots(n, buf):
        slabs, start = item_coords(n)
        pvs = []
        for s, slab in enumerate(slabs):
            v2 = v_ref[0, slab, pl.ds(start, K_TILE), :]
            zero = jnp.zeros_like(v2)
            pvs.append(jnp.dot(buf[2][HEADS_PER_SLAB * s], jnp.where(lo, v2, zero), preferred_element_type=F32)
                       + jnp.dot(buf[2][HEADS_PER_SLAB * s + 1], jnp.where(lo, zero, v2),
                                 preferred_element_type=F32))
        return pvs

    def stage4(n, buf, pvs, first):
        slabs, _ = item_coords(n)
        for s, slab in enumerate(slabs):
            if first:
                o_scr[o_base + slab] = pvs[s]
            else:
                o_scr[o_base + slab] = o_scr[o_base + slab] + pvs[s] * buf[3][s]

    def step(m, cur, nxt, static):
        pvs = pv_dots(m, cur)
        incl = suffix_sums(nxt)
        stage3(m + 1, nxt, incl, static and m + 1 < GROUPS)
        qk_logits(m + 3, nxt)
        stage2(cur, static and m + 2 < GROUPS)
        stage4(m, cur, pvs, static and m < GROUPS - 1)

    qk_logits(0, buf_a)
    qk_logits(1, buf_b)
    stage2(buf_a, True)
    stage3(0, buf_a, suffix_sums(buf_a), True)
    qk_logits(2, buf_a)
    stage2(buf_b, True)
    for m in range(GROUPS - 1):
        cur, nxt = (buf_a, buf_b) if m % 2 == 0 else (buf_b, buf_a)
        step(m, cur, nxt, True)

    def step_group(t, _):
        m = GROUPS - 1 + GROUPS * t
        for r in range(0, GROUPS, 2):
            step(m + r, buf_b, buf_a, False)
            step(m + r + 1, buf_a, buf_b, False)
        return 0

    def step_group_pair(t, _):
        step_group(2 * t, 0)
        step_group(2 * t + 1, 0)
        return 0

    lax.fori_loop(0, i // 2, step_group_pair, 0)

    @pl.when(i % 2 == 1)
    def _():
        step_group(i - 1, 0)

    last = GROUPS * i + GROUPS - 1
    stage4(last, buf_b, pv_dots(last, buf_b), False)

    @pl.when(lax.rem(i, TILES_PER_EPILOGUE) == TILES_PER_EPILOGUE - 1)
    def _():
        o = jnp.concatenate(
            [jnp.concatenate([o_scr[t * N_SLABS + c] for c in range(N_SLABS)], axis=-1)
             for t in range(TILES_PER_EPILOGUE)], axis=0)
        z = z_ref[0]
        gated = o * (z * _sigmoid(z))
        x = x1_ref[0] + jnp.dot(gated.astype(BF16), w_out_ref[...], preferred_element_type=F32)
        pe = jnp.dot(p_ref[0, 0].astype(BF16), ple_w_ref[...], preferred_element_type=F32)
        gate = _sigmoid(jnp.dot(x.astype(BF16), gate_w_ref[...], preferred_element_type=F32))
        out_ref[0] = x + pe * gate


def _resident(shape):
    return pl.BlockSpec(shape, lambda b, t: (0,) * len(shape), pipeline_mode=pl.Buffered(1))


def kernel(x, p, a_norm, a_w_in, a_w_group, a_scale, a_w_out, kv_norm, w_kv, k_norm, b_norm, b_w_in, b_q_norm,
           b_w_out, ple_w, ple_gate_w):
    B, S, D = x.shape
    assert D == D_MODEL and S % ROW_TILE == 0 and S % Q_TILE == 0 and Q_TILE == K_TILE
    assert GROUPS % 2 == 0 and (S // Q_TILE) % TILES_PER_EPILOGUE == 0
    assert a_norm.shape[0] == 1 and b_norm.shape[0] == 1 and p.shape[0] == 2

    per_head = lambda a: jnp.tile(a, SB_HEADS)
    vec_rows = [a_norm[0], a_scale[0], kv_norm, b_norm[0], per_head(k_norm), per_head(b_q_norm[0])]
    vecs = jnp.stack([r.astype(F32) for r in vec_rows] + [jnp.zeros((D_MODEL,), F32)] * (VEC_ROWS - len(vec_rows)))
    wb = lambda a: a.astype(BF16)

    slab_shape = jax.ShapeDtypeStruct((B, N_SLABS, S, LANES), BF16)
    tok_spec = pl.BlockSpec((1, ROW_TILE, D_MODEL), lambda b, t: (b, t, 0))
    slab_spec = pl.BlockSpec((1, N_SLABS, ROW_TILE, LANES), lambda b, t: (b, 0, t, 0))

    x1, q, k, v, z = pl.pallas_call(
        _layer0_kernel,
        grid=(B, S // ROW_TILE),
        in_specs=[
            tok_spec,
            pl.BlockSpec((1, 1, ROW_TILE, PLE_DIM), lambda b, t: (0, b, t, 0)),
            _resident((VEC_ROWS, D_MODEL)),
        ] + [pl.BlockSpec(memory_space=pl.ANY)] * 7,
        out_specs=[tok_spec, slab_spec, slab_spec, slab_spec, tok_spec],
        out_shape=[jax.ShapeDtypeStruct((B, S, D_MODEL), F32), slab_shape, slab_shape, slab_shape,
                   jax.ShapeDtypeStruct((B, S, D_MODEL), F32)],
        scratch_shapes=[pltpu.VMEM((POOL_HALO, D_MODEL), F32),
                        pltpu.VMEM((D_MODEL, 2 * D_MODEL), BF16),
                        pltpu.VMEM((len(POOL_WINDOWS), POOL_GROUP_DIM, POOL_GROUP_DIM), BF16),
                        pltpu.VMEM((D_MODEL, D_MODEL), BF16),
                        pltpu.VMEM((PLE_DIM, D_MODEL), BF16),
                        pltpu.VMEM((D_MODEL, D_MODEL), BF16),
                        pltpu.VMEM((D_MODEL, 2 * D_MODEL), BF16),
                        pltpu.VMEM((D_MODEL, 2 * D_MODEL), BF16),
                        pltpu.VMEM((2, WEIGHT_CHUNK_ROWS, 2 * D_MODEL), F32),
                        pltpu.SemaphoreType.DMA((2,))],
        compiler_params=pltpu.CompilerParams(
            dimension_semantics=("arbitrary", "arbitrary"), vmem_limit_bytes=VMEM_LIMIT_BYTES),
        name="layer0_and_projections",
    )(x, p, vecs, *(w.astype(F32) for w in (a_w_in, a_w_group, a_w_out, ple_w, ple_gate_w, w_kv, b_w_in)))

    tri = (lax.broadcasted_iota(jnp.int32, (K_TILE, K_TILE), 0)
           >= lax.broadcasted_iota(jnp.int32, (K_TILE, K_TILE), 1)).astype(BF16)
    qtok_spec = pl.BlockSpec((1, TILES_PER_EPILOGUE * Q_TILE, D_MODEL), lambda b, i: (b, i // TILES_PER_EPILOGUE, 0))
    seq_spec = pl.BlockSpec((1, N_SLABS, S, LANES), lambda b, i: (b, 0, 0, 0))
    stage_buffers = [pltpu.VMEM((HEADS_PER_ITEM, Q_TILE, K_TILE), F32),
                     pltpu.VMEM((HEADS_PER_ITEM, Q_TILE, K_TILE), BF16),
                     pltpu.VMEM((HEADS_PER_ITEM, Q_TILE, K_TILE), BF16),
                     pltpu.VMEM((SLABS_PER_ITEM, Q_TILE, LANES), F32)]

    return pl.pallas_call(
        _attn_kernel,
        grid=(B, S // Q_TILE),
        in_specs=[
            pl.BlockSpec((1, N_SLABS, Q_TILE, LANES), lambda b, i: (b, 0, i, 0)),
            seq_spec, seq_spec,
            qtok_spec, qtok_spec,
            pl.BlockSpec((1, 1, TILES_PER_EPILOGUE * Q_TILE, PLE_DIM), lambda b, i: (1, b, i // TILES_PER_EPILOGUE, 0)),
            _resident((K_TILE, K_TILE)),
            _resident((D_MODEL, D_MODEL)),
            _resident((PLE_DIM, D_MODEL)),
            _resident((D_MODEL, D_MODEL)),
        ],
        out_specs=qtok_spec,
        out_shape=jax.ShapeDtypeStruct((B, S, D_MODEL), F32),
        scratch_shapes=[pltpu.VMEM((TILES_PER_EPILOGUE * N_SLABS, Q_TILE, LANES), F32),
                        pltpu.VMEM((N_SLABS, Q_TILE, LANES), F32),
                        pltpu.VMEM((SB_HEADS, Q_TILE, LANES), BF16)]
                       + stage_buffers + stage_buffers,
        compiler_params=pltpu.CompilerParams(
            dimension_semantics=("arbitrary", "arbitrary"), vmem_limit_bytes=VMEM_LIMIT_BYTES),
        name="stick_breaking_layer",
    )(q, k, v, z, x1, p, tri, wb(b_w_out[0]), wb(ple_w[1]), wb(ple_gate_w[1]))
```

```python
import jax
import jax.numpy as jnp
from jax import lax
from jax.experimental import pallas as pl
from jax.experimental.pallas import tpu as pltpu

D_MODEL = 1024
PLE_DIM = 256
POOL_WINDOWS = (2, 4, 8, 16)
POOL_GROUP_DIM = D_MODEL // len(POOL_WINDOWS)
POOL_HALO = 16
SB_HEADS = 16
SB_HEAD_DIM = D_MODEL // SB_HEADS
SB_SCALE = SB_HEAD_DIM ** -0.5
EPS = 1e-6
LOG2_E = 1.4426950408889634

LANES = 128
HEADS_PER_SLAB = LANES // SB_HEAD_DIM
N_SLABS = D_MODEL // LANES

ROW_TILE = 512
Q_TILE = 256
K_TILE = 256
SLABS_PER_ITEM = 1
HEADS_PER_ITEM = SLABS_PER_ITEM * HEADS_PER_SLAB
GROUPS = N_SLABS // SLABS_PER_ITEM
TILES_PER_EPILOGUE = 2
MASKED_LOG2 = -1e30
WEIGHT_CHUNK_ROWS = 256
WEIGHT_SLOTS = 3
VEC_ROWS = 8
VMEM_LIMIT_BYTES = 56 * 1024 * 1024

F32 = jnp.float32
BF16 = jnp.bfloat16


def _sigmoid(t):
    return 1.0 / (1.0 + jnp.exp(-t))


def _rms_scale(t):
    return lax.rsqrt(jnp.mean(t * t, axis=-1, keepdims=True) + EPS)


def _head_rms(t):
    rows = t.shape[0]
    lo = lax.broadcasted_iota(jnp.int32, (rows, LANES), 1) < SB_HEAD_DIM
    outs = []
    for c in range(N_SLABS):
        blk = t[:, c * LANES:(c + 1) * LANES]
        sq = blk * blk
        s_lo = jnp.sum(jnp.where(lo, sq, 0.0), axis=-1, keepdims=True)
        s_hi = jnp.sum(jnp.where(lo, 0.0, sq), axis=-1, keepdims=True)
        ms = jnp.where(lo, s_lo, s_hi) * (1.0 / SB_HEAD_DIM)
        outs.append(blk * lax.rsqrt(ms + EPS))
    return jnp.concatenate(outs, axis=-1)


def _load_weights(pairs, stage_ref, sems):
    n_slots = stage_ref.shape[0]
    chunks = []
    for src, dst in pairs:
        n_rows, n_cols = dst.shape
        for r0 in range(0, n_rows, WEIGHT_CHUNK_ROWS):
            nr = min(WEIGHT_CHUNK_ROWS, n_rows - r0)
            chunks.append((src.at[pl.ds(r0, nr), :], dst.at[pl.ds(r0, nr), :], nr, n_cols))

    def copy(c):
        src, _, nr, nc = chunks[c]
        slot = c % n_slots
        return pltpu.make_async_copy(src, stage_ref.at[slot, pl.ds(0, nr), pl.ds(0, nc)], sems.at[slot])

    for c in range(min(n_slots - 1, len(chunks))):
        copy(c).start()
    for c, (_, dst, nr, nc) in enumerate(chunks):
        ahead = c + n_slots - 1
        if ahead < len(chunks):
            copy(ahead).start()
        copy(c).wait()
        dst[...] = stage_ref[c % n_slots, :nr, :nc].astype(BF16)


def _layer0_kernel(x_ref, p_ref, vec_ref,
                   w_in_hbm, w_group_hbm, w_out_hbm, ple_w_hbm, gate_w_hbm, w_kv_hbm, bw_in_hbm,
                   x1_ref, q_ref, k_ref, v_ref, z_ref, halo_ref,
                   w_in_ref, w_group_ref, w_out_ref, ple_w_ref, gate_w_ref, w_kv_ref, bw_in_ref, stage_ref, sems):
    t = pl.program_id(1)
    rows = x_ref.shape[1]

    @pl.when((pl.program_id(0) == 0) & (t == 0))
    def _():
        pairs = [(w_in_hbm.at[0], w_in_ref), (w_out_hbm.at[0], w_out_ref), (ple_w_hbm.at[0], ple_w_ref),
                 (gate_w_hbm.at[0], gate_w_ref), (w_kv_hbm, w_kv_ref), (bw_in_hbm.at[0], bw_in_ref)]
        pairs += [(w_group_hbm.at[0, g], w_group_ref.at[g]) for g in range(len(POOL_WINDOWS))]
        _load_weights(pairs, stage_ref, sems)

    a_norm, a_scale, kv_norm, b_norm, k_gain, q_gain = (vec_ref[r:r + 1, :] for r in range(6))

    @pl.when(t == 0)
    def _():
        halo_ref[...] = jnp.zeros_like(halo_ref)

    x = x_ref[0]
    h = x * _rms_scale(x) * a_norm
    uz = jnp.dot(h.astype(BF16), w_in_ref[...], preferred_element_type=F32)
    u = uz[:, :D_MODEL]
    z = uz[:, D_MODEL:]

    ext = jnp.concatenate([halo_ref[...], u], axis=0)
    halo_ref[...] = u[rows - POOL_HALO:, :]
    pos1 = t * rows + lax.broadcasted_iota(jnp.int32, (rows, 1), 0) + 1
    mixed = []
    for g, w in enumerate(POOL_WINDOWS):
        e = ext[:, g * POOL_GROUP_DIM:(g + 1) * POOL_GROUP_DIM]
        s = e
        shift = 1
        while shift < w:
            s = s + pltpu.roll(s, shift, axis=0)
            shift *= 2
        inv_cnt = 1.0 / jnp.minimum(pos1, w).astype(F32)
        pooled = s[POOL_HALO:, :] * inv_cnt - e[POOL_HALO:, :]
        mixed.append(jnp.dot(pooled.astype(BF16), w_group_ref[g], preferred_element_type=F32))
    mixed = jnp.concatenate(mixed, axis=-1) * a_scale
    gated = mixed * (z * _sigmoid(z))
    x = x + jnp.dot(gated.astype(BF16), w_out_ref[...], preferred_element_type=F32)

    pe = jnp.dot(p_ref[0, 0].astype(BF16), ple_w_ref[...], preferred_element_type=F32)
    gate = _sigmoid(jnp.dot(x.astype(BF16), gate_w_ref[...], preferred_element_type=F32))
    x = x + pe * gate
    x1_ref[0] = x

    xn = x * _rms_scale(x)
    kv = jnp.dot((xn * kv_norm).astype(BF16), w_kv_ref[...], preferred_element_type=F32)
    qz = jnp.dot((xn * b_norm).astype(BF16), bw_in_ref[...], preferred_element_type=F32)
    k = (_head_rms(kv[:, :D_MODEL]) * k_gain).astype(BF16)
    v = kv[:, D_MODEL:].astype(BF16)
    q = (_head_rms(qz[:, :D_MODEL]) * q_gain * (SB_SCALE * LOG2_E)).astype(BF16)
    z_ref[0] = qz[:, D_MODEL:]
    for c in range(N_SLABS):
        sl = slice(c * LANES, (c + 1) * LANES)
        q_ref[0, c] = q[:, sl]
        k_ref[0, c] = k[:, sl]
        v_ref[0, c] = v[:, sl]


def _attn_kernel(q_ref, k_ref, v_ref, z_ref, x1_ref, p_ref, tri_ref, w_out_hbm, ple_w_hbm, gate_w_hbm,
                 out_ref, o_scr, c_scr, qm_scr, lg_a, sp_a, a_a, dec_a, lg_b, sp_b, a_b, dec_b,
                 w_out_ref, ple_w_ref, gate_w_ref, stage_ref, sems):
    i = pl.program_id(1)
    tq = q_ref.shape[2]

    @pl.when((pl.program_id(0) == 0) & (i == 0))
    def _():
        _load_weights([(w_out_hbm.at[0], w_out_ref), (ple_w_hbm.at[1], ple_w_ref), (gate_w_hbm.at[1], gate_w_ref)],
                      stage_ref, sems)

    lo = lax.broadcasted_iota(jnp.int32, (1, LANES), 1) < SB_HEAD_DIM
    row = lax.broadcasted_iota(jnp.int32, (tq, K_TILE), 0)
    col = lax.broadcasted_iota(jnp.int32, (tq, K_TILE), 1)
    causal = col < row
    tri = tri_ref[...]
    buf_a = (lg_a, sp_a, a_a, dec_a)
    buf_b = (lg_b, sp_b, a_b, dec_b)

    o_base = lax.rem(i, TILES_PER_EPILOGUE) * N_SLABS
    for slab in range(N_SLABS - SLABS_PER_ITEM, N_SLABS):
        o_scr[o_base + slab] = jnp.zeros(o_scr.shape[1:], F32)
    for slab in range(N_SLABS):
        q2 = q_ref[0, slab]
        zero = jnp.zeros_like(q2)
        qm_scr[HEADS_PER_SLAB * slab] = jnp.where(lo, q2, zero)
        qm_scr[HEADS_PER_SLAB * slab + 1] = jnp.where(lo, zero, q2)

    def item_coords(n):
        if isinstance(n, int):
            group, back = n % GROUPS, n // GROUPS
        else:
            group, back = lax.rem(n, GROUPS), lax.div(n, GROUPS)
        start = pl.multiple_of(jnp.maximum(i - back, 0) * K_TILE, K_TILE)
        return [SLABS_PER_ITEM * group + s for s in range(SLABS_PER_ITEM)], start

    def qk_logits(n, buf, diagonal):
        slabs, start = item_coords(n)
        for s, slab in enumerate(slabs):
            k2 = k_ref[0, slab, pl.ds(start, K_TILE), :]
            for hh in range(HEADS_PER_SLAB):
                l = lax.dot_general(qm_scr[HEADS_PER_SLAB * slab + hh], k2, (((1,), (1,)), ((), ())),
                                    preferred_element_type=F32)
                buf[0][HEADS_PER_SLAB * s + hh] = jnp.where(causal, l, MASKED_LOG2) if diagonal else l

    def stage2(buf):
        lg_buf, sp_buf = buf[0], buf[1]
        for h in range(HEADS_PER_ITEM):
            l = lg_buf[h]
            sp_buf[h] = (jnp.maximum(l, 0.0) + jnp.log2(1.0 + jnp.exp2(-jnp.abs(l)))).astype(BF16)

    def suffix_sums(buf):
        return [jnp.dot(buf[1][h], tri, preferred_element_type=F32) for h in range(HEADS_PER_ITEM)]

    def stage3(n, buf, incl, first):
        lg_buf, _, a_buf, dec_buf = buf
        for h in range(HEADS_PER_ITEM):
            a_buf[h] = jnp.exp2(lg_buf[h] - incl[h]).astype(BF16)
        slabs, _ = item_coords(n)
        for s, slab in enumerate(slabs):
            row_sum = jnp.where(lo, incl[HEADS_PER_SLAB * s][:, 0:1], incl[HEADS_PER_SLAB * s + 1][:, 0:1])
            if first:
                dec_buf[s] = jnp.ones(dec_buf.shape[1:], F32)
                c_scr[slab] = row_sum
            else:
                c = c_scr[slab]
                dec_buf[s] = jnp.exp2(-c)
                c_scr[slab] = c + row_sum

    def pv_dots(n, buf):
        slabs, start = item_coords(n)
        pvs = []
        for s, slab in enumerate(slabs):
            v2 = v_ref[0, slab, pl.ds(start, K_TILE), :]
            zero = jnp.zeros_like(v2)
            pvs.append(jnp.dot(buf[2][HEADS_PER_SLAB * s], jnp.where(lo, v2, zero), preferred_element_type=F32)
                       + jnp.dot(buf[2][HEADS_PER_SLAB * s + 1], jnp.where(lo, zero, v2),
                                 preferred_element_type=F32))
        return pvs

    def stage4(n, buf, pvs, first):
        slabs, _ = item_coords(n)
        for s, slab in enumerate(slabs):
            if first:
                o_scr[o_base + slab] = pvs[s]
            else:
                o_scr[o_base + slab] = o_scr[o_base + slab] + pvs[s] * buf[3][s]

    def step(m, cur, nxt, static):
        pvs = pv_dots(m, cur)
        incl = suffix_sums(nxt)
        stage3(m + 1, nxt, incl, static and m + 1 < GROUPS)
        qk_logits(m + 3, nxt, static and m + 3 < GROUPS)
        stage2(cur)
        stage4(m, cur, pvs, static and m < GROUPS - 1)

    qk_logits(0, buf_a, True)
    qk_logits(1, buf_b, True)
    stage2(buf_a)
    stage3(0, buf_a, suffix_sums(buf_a), True)
    qk_logits(2, buf_a, True)
    stage2(buf_b)
    for m in range(GROUPS - 1):
        cur, nxt = (buf_a, buf_b) if m % 2 == 0 else (buf_b, buf_a)
        step(m, cur, nxt, True)

    def step_group(t, _):
        m = GROUPS - 1 + GROUPS * t
        for r in range(0, GROUPS, 2):
            step(m + r, buf_b, buf_a, False)
            step(m + r + 1, buf_a, buf_b, False)
        return 0

    def step_group_pair(t, _):
        step_group(2 * t, 0)
        step_group(2 * t + 1, 0)
        return 0

    lax.fori_loop(0, i // 2, step_group_pair, 0)

    @pl.when(i % 2 == 1)
    def _():
        step_group(i - 1, 0)

    last = GROUPS * i + GROUPS - 1
    stage4(last, buf_b, pv_dots(last, buf_b), False)

    @pl.when(lax.rem(i, TILES_PER_EPILOGUE) == TILES_PER_EPILOGUE - 1)
    def _():
        o = jnp.concatenate(
            [jnp.concatenate([o_scr[t * N_SLABS + c] for c in range(N_SLABS)], axis=-1)
             for t in range(TILES_PER_EPILOGUE)], axis=0)
        z = z_ref[0]
        gated = o * (z * _sigmoid(z))
        x = x1_ref[0] + jnp.dot(gated.astype(BF16), w_out_ref[...], preferred_element_type=F32)
        pe = jnp.dot(p_ref[0, 0].astype(BF16), ple_w_ref[...], preferred_element_type=F32)
        gate = _sigmoid(jnp.dot(x.astype(BF16), gate_w_ref[...], preferred_element_type=F32))
        out_ref[0] = x + pe * gate


def _resident(shape):
    return pl.BlockSpec(shape, lambda b, t: (0,) * len(shape), pipeline_mode=pl.Buffered(1))


def kernel(x, p, a_norm, a_w_in, a_w_group, a_scale, a_w_out, kv_norm, w_kv, k_norm, b_norm, b_w_in, b_q_norm,
           b_w_out, ple_w, ple_gate_w):
    B, S, D = x.shape
    assert D == D_MODEL and S % ROW_TILE == 0 and S % Q_TILE == 0 and Q_TILE == K_TILE
    assert GROUPS % 2 == 0 and (S // Q_TILE) % TILES_PER_EPILOGUE == 0
    assert a_norm.shape[0] == 1 and b_norm.shape[0] == 1 and p.shape[0] == 2

    per_head = lambda a: jnp.tile(a, SB_HEADS)
    vec_rows = [a_norm[0], a_scale[0], kv_norm, b_norm[0], per_head(k_norm), per_head(b_q_norm[0])]
    vecs = jnp.stack([r.astype(F32) for r in vec_rows] + [jnp.zeros((D_MODEL,), F32)] * (VEC_ROWS - len(vec_rows)))

    slab_shape = jax.ShapeDtypeStruct((B, N_SLABS, S, LANES), BF16)
    tok_spec = pl.BlockSpec((1, ROW_TILE, D_MODEL), lambda b, t: (b, t, 0))
    slab_spec = pl.BlockSpec((1, N_SLABS, ROW_TILE, LANES), lambda b, t: (b, 0, t, 0))

    x1, q, k, v, z = pl.pallas_call(
        _layer0_kernel,
        grid=(B, S // ROW_TILE),
        in_specs=[
            tok_spec,
            pl.BlockSpec((1, 1, ROW_TILE, PLE_DIM), lambda b, t: (0, b, t, 0)),
            _resident((VEC_ROWS, D_MODEL)),
        ] + [pl.BlockSpec(memory_space=pl.ANY)] * 7,
        out_specs=[tok_spec, slab_spec, slab_spec, slab_spec, tok_spec],
        out_shape=[jax.ShapeDtypeStruct((B, S, D_MODEL), F32), slab_shape, slab_shape, slab_shape,
                   jax.ShapeDtypeStruct((B, S, D_MODEL), F32)],
        scratch_shapes=[pltpu.VMEM((POOL_HALO, D_MODEL), F32),
                        pltpu.VMEM((D_MODEL, 2 * D_MODEL), BF16),
                        pltpu.VMEM((len(POOL_WINDOWS), POOL_GROUP_DIM, POOL_GROUP_DIM), BF16),
                        pltpu.VMEM((D_MODEL, D_MODEL), BF16),
                        pltpu.VMEM((PLE_DIM, D_MODEL), BF16),
                        pltpu.VMEM((D_MODEL, D_MODEL), BF16),
                        pltpu.VMEM((D_MODEL, 2 * D_MODEL), BF16),
                        pltpu.VMEM((D_MODEL, 2 * D_MODEL), BF16),
                        pltpu.VMEM((WEIGHT_SLOTS, WEIGHT_CHUNK_ROWS, 2 * D_MODEL), F32),
                        pltpu.SemaphoreType.DMA((WEIGHT_SLOTS,))],
        compiler_params=pltpu.CompilerParams(
            dimension_semantics=("arbitrary", "arbitrary"), vmem_limit_bytes=VMEM_LIMIT_BYTES),
        name="layer0_and_projections",
    )(x, p, vecs, *(w.astype(F32) for w in (a_w_in, a_w_group, a_w_out, ple_w, ple_gate_w, w_kv, b_w_in)))

    tri = (lax.broadcasted_iota(jnp.int32, (K_TILE, K_TILE), 0)
           >= lax.broadcasted_iota(jnp.int32, (K_TILE, K_TILE), 1)).astype(BF16)
    qtok_spec = pl.BlockSpec((1, TILES_PER_EPILOGUE * Q_TILE, D_MODEL), lambda b, i: (b, i // TILES_PER_EPILOGUE, 0))
    seq_spec = pl.BlockSpec((1, N_SLABS, S, LANES), lambda b, i: (b, 0, 0, 0))
    stage_buffers = [pltpu.VMEM((HEADS_PER_ITEM, Q_TILE, K_TILE), F32),
                     pltpu.VMEM((HEADS_PER_ITEM, Q_TILE, K_TILE), BF16),
                     pltpu.VMEM((HEADS_PER_ITEM, Q_TILE, K_TILE), BF16),
                     pltpu.VMEM((SLABS_PER_ITEM, Q_TILE, LANES), F32)]

    return pl.pallas_call(
        _attn_kernel,
        grid=(B, S // Q_TILE),
        in_specs=[
            pl.BlockSpec((1, N_SLABS, Q_TILE, LANES), lambda b, i: (b, 0, i, 0)),
            seq_spec, seq_spec,
            qtok_spec, qtok_spec,
            pl.BlockSpec((1, 1, TILES_PER_EPILOGUE * Q_TILE, PLE_DIM), lambda b, i: (1, b, i // TILES_PER_EPILOGUE, 0)),
            _resident((K_TILE, K_TILE)),
        ] + [pl.BlockSpec(memory_space=pl.ANY)] * 3,
        out_specs=qtok_spec,
        out_shape=jax.ShapeDtypeStruct((B, S, D_MODEL), F32),
        scratch_shapes=[pltpu.VMEM((TILES_PER_EPILOGUE * N_SLABS, Q_TILE, LANES), F32),
                        pltpu.VMEM((N_SLABS, Q_TILE, LANES), F32),
                        pltpu.VMEM((SB_HEADS, Q_TILE, LANES), BF16)]
                       + stage_buffers + stage_buffers
                       + [pltpu.VMEM((D_MODEL, D_MODEL), BF16), pltpu.VMEM((PLE_DIM, D_MODEL), BF16),
                          pltpu.VMEM((D_MODEL, D_MODEL), BF16),
                          pltpu.VMEM((WEIGHT_SLOTS, WEIGHT_CHUNK_ROWS, D_MODEL), F32),
                          pltpu.SemaphoreType.DMA((WEIGHT_SLOTS,))],
        compiler_params=pltpu.CompilerParams(
            dimension_semantics=("arbitrary", "arbitrary"), vmem_limit_bytes=VMEM_LIMIT_BYTES),
        name="stick_breaking_layer",
    )(q, k, v, z, x1, p, tri, b_w_out.astype(F32), ple_w.astype(F32), ple_gate_w.astype(F32))
```
